```python
import jax, jax.numpy as jnp
from jax import lax
import numpy as np

D_MODEL = 2048
BATCH = 8
SEQ = 2048
DEPTH = 1

CTX_LEN = 256
GRID_W = 64
RET_HEADS = 8
RET_DK = 256
RET_DV = 256
RET_QK = RET_HEADS * RET_DK
RET_VW = RET_HEADS * RET_DV
RET_CHUNK = 128
ROPE_FREQS = RET_DK // 4
ROPE_BASE = 10000.0
SG_GROUPS = 8
SG_CHUNK = 128
SG_WIDTH = 2048
SG_GD = SG_WIDTH // SG_GROUPS
FFN_HIDDEN = -(-8 * D_MODEL // (3 * 256)) * 256
EPS = 1e-6
Q_OFF = 0
K_OFF = Q_OFF + RET_QK
V_OFF = K_OFF + RET_QK
G_OFF = V_OFF + RET_VW
U_OFF = G_OFF + RET_VW
VS_OFF = U_OFF + SG_WIDTH
GR_OFF = VS_OFF + SG_WIDTH
GS_OFF = GR_OFF + D_MODEL
D_IN = GS_OFF + D_MODEL

kernel_name = "hybrid_retention_gmlp_prefix_dit"

F32 = jnp.float32


def rmsnorm(x, g):
    xf = x.astype(F32)
    y = xf * lax.rsqrt(jnp.mean(xf * xf, axis=-1, keepdims=True) + EPS)
    return (y * g.astype(F32)).astype(x.dtype)


def layernorm(x, g, b):
    xf = x.astype(F32)
    mu = jnp.mean(xf, axis=-1, keepdims=True)
    var = jnp.mean(jnp.square(xf - mu), axis=-1, keepdims=True)
    y = (xf - mu) * lax.rsqrt(var + EPS)
    return (y * g.astype(F32) + b.astype(F32)).astype(x.dtype)


def adaln(cond, w_mod, b_mod):
    return jnp.split(jax.nn.silu(cond) @ w_mod + b_mod, 6, axis=-1)


def modulate(h, shift, scale):
    return h * (1.0 + scale) + shift


def rope_tables(L):
    rows = L // GRID_W
    row = jnp.repeat(jnp.arange(rows), GRID_W)
    col = jnp.tile(jnp.arange(GRID_W), rows)
    freq = ROPE_BASE ** (-jnp.arange(ROPE_FREQS, dtype=F32) / ROPE_FREQS)
    ang = jnp.stack([row, col], axis=-1).astype(F32)[:, :, None] * freq
    return jnp.cos(ang), jnp.sin(ang)


def apply_rope(x, cos, sin):
    B, L, H, Dk = x.shape
    xb = x.reshape(B, L, H, 2, 2, ROPE_FREQS)
    x1, x2 = xb[..., 0, :], xb[..., 1, :]
    c = cos[None, :, None]
    s = sin[None, :, None]
    return jnp.stack([x1 * c - x2 * s, x2 * c + x1 * s], axis=-2).reshape(B, L, H, Dk).astype(x.dtype)


def log_decay(logit):
    return -jax.nn.softplus(-logit.astype(F32))


def retention_scan(q, k, v, lg, s0, include_diag):
    B, H, L, _ = q.shape
    Dv = v.shape[-1]
    C = RET_CHUNK
    n = L // C

    def chunks(t):
        return jnp.moveaxis(t.reshape(B, H, n, C, t.shape[-1]), 2, 0)

    idx = jnp.arange(C, dtype=F32)
    diff = idx[:, None] - idx[None, :]
    mask = (diff >= 0) if include_diag else (diff > 0)
    decay_in = jnp.where(mask, jnp.exp(lg[:, None, None] * jnp.maximum(diff, 0.0)), 0.0)
    q_dec = jnp.exp(lg[:, None] * (idx + 1.0))[None, :, :, None]
    k_dec = jnp.exp(lg[:, None] * (C - 1.0 - idx))[None, :, :, None]
    c_dec = jnp.exp(lg * C)[None, :, None, None]

    def step(S, qkv):
        qc, kc, vc = qkv
        scores = jnp.einsum('bhid,bhjd->bhij', qc, kc) * decay_in
        out = (jnp.einsum('bhij,bhje->bhie', scores, vc)
               + jnp.einsum('bhid,bhde->bhie', qc, S) * q_dec)
        S = S * c_dec + jnp.einsum('bhjd,bhje->bhde', kc * k_dec, vc)
        return S, out

    _, out = lax.scan(step, s0, (chunks(q), chunks(k), chunks(v)))
    return jnp.moveaxis(out, 0, 2).reshape(B, H, L, Dv)


def bidirectional_retention(q, k, v, lg_f, lg_b, s_f, s_b):
    q, k, v = (jnp.swapaxes(t.astype(F32), 1, 2) for t in (q, k, v))
    flip = lambda t: jnp.flip(t, axis=2)
    fwd = retention_scan(q, k, v, lg_f, s_f, True)
    bwd = flip(retention_scan(flip(q), flip(k), flip(v), lg_b, s_b, False))
    return jnp.swapaxes(fwd + bwd, 1, 2)


def context_states(k, v, lg_f, lg_b):
    Lc = k.shape[2]
    j = jnp.arange(Lc, dtype=F32)
    w_f = jnp.exp(lg_f[:, None] * (Lc - 1.0 - j))[None, :, :, None]
    w_b = jnp.exp(lg_b[:, None] * j)[None, :, :, None]
    s_f = jnp.einsum('bhld,bhle->bhde', k * w_f, v)
    s_b = jnp.einsum('bhld,bhle->bhde', k * w_b, v)
    return s_f, s_b


def spatial_gating(u, vs, ln_g, ln_b, w_s, b_s):
    B, L, _ = u.shape
    n = L // SG_CHUNK
    vn = layernorm(vs, ln_g, ln_b).reshape(B, n, SG_CHUNK, SG_GROUPS, SG_GD)
    mixed = jnp.einsum('gij,bnjgd->bnigd', w_s, vn) + b_s.T[None, None, :, :, None]
    return u * mixed.reshape(B, L, SG_WIDTH)


def token_mixer(h, s_f, s_b, rope, w_in, lg_f, lg_b, sg_ln_g, sg_ln_b, sg_w, sg_b,
                w_ret_o, w_sg_o, w_out):
    B, L, _ = h.shape
    p = h @ w_in
    q = p[..., Q_OFF:K_OFF].reshape(B, L, RET_HEADS, RET_DK)
    k = p[..., K_OFF:V_OFF].reshape(B, L, RET_HEADS, RET_DK) * (RET_DK ** -0.5)
    v = p[..., V_OFF:G_OFF].reshape(B, L, RET_HEADS, RET_DV)
    g_ret = p[..., G_OFF:U_OFF]
    u = jax.nn.gelu(p[..., U_OFF:VS_OFF])
    vs = jax.nn.gelu(p[..., VS_OFF:GR_OFF])
    gate_r = jax.nn.sigmoid(p[..., GR_OFF:GS_OFF].astype(F32))
    gate_s = jax.nn.sigmoid(p[..., GS_OFF:D_IN].astype(F32))
    if rope is not None:
        q = apply_rope(q, *rope)
        k = apply_rope(k, *rope)
    ret = bidirectional_retention(q, k, v, lg_f, lg_b, s_f, s_b)
    ret = ret * lax.rsqrt(jnp.mean(ret * ret, axis=-1, keepdims=True) + EPS)
    ret = ret.reshape(B, L, RET_VW) * jax.nn.silu(g_ret)
    y_ret = ret @ w_ret_o
    y_sg = spatial_gating(u, vs, sg_ln_g, sg_ln_b, sg_w, sg_b) @ w_sg_o
    return (gate_r * y_ret + gate_s * y_sg) @ w_out


def swiglu(h, w_ffn_in, w_ffn_out):
    a, b = jnp.split(h @ w_ffn_in, 2, axis=-1)
    return (jax.nn.silu(a) * b) @ w_ffn_out


def setup_inputs(seed: int = 0) -> dict:
    key = jax.random.key(seed)
    ks = jax.random.split(key, 24)
    nrm = lambda k, shape, s: jax.random.normal(k, shape, F32) * s
    base_logit = jnp.log(2.0 ** (5.0 + jnp.arange(RET_HEADS, dtype=F32)) - 1.0)
    return {
        "x": nrm(ks[0], (BATCH, SEQ, D_MODEL), 1.0),
        "c": nrm(ks[1], (BATCH, D_MODEL), 1.0),
        "ctx": nrm(ks[2], (BATCH, CTX_LEN, D_MODEL), 1.0),
        "c_ctx": nrm(ks[3], (D_MODEL,), 1.0),
        "w_mod": nrm(ks[4], (DEPTH, D_MODEL, 6 * D_MODEL), 0.5 * D_MODEL ** -0.5),
        "b_mod": nrm(ks[5], (DEPTH, 6 * D_MODEL), 0.01),
        "norm1_g": 1.0 + nrm(ks[6], (DEPTH, D_MODEL), 0.02),
        "w_in": nrm(ks[7], (DEPTH, D_MODEL, D_IN), D_MODEL ** -0.5),
        "ret_decay_fwd": base_logit[None] + nrm(ks[8], (DEPTH, RET_HEADS), 0.1),
        "ret_decay_bwd": base_logit[None] + nrm(ks[9], (DEPTH, RET_HEADS), 0.1),
        "sg_ln_g": 1.0 + nrm(ks[10], (DEPTH, SG_WIDTH), 0.02),
        "sg_ln_b": nrm(ks[11], (DEPTH, SG_WIDTH), 0.02),
        "sg_w": nrm(ks[12], (DEPTH, SG_GROUPS, SG_CHUNK, SG_CHUNK), 0.5 * SG_CHUNK ** -0.5),
        "sg_b": 1.0 + nrm(ks[13], (DEPTH, SG_GROUPS, SG_CHUNK), 0.02),
        "w_ret_o": nrm(ks[14], (DEPTH, RET_VW, D_MODEL), RET_VW ** -0.5),
        "w_sg_o": nrm(ks[15], (DEPTH, SG_WIDTH, D_MODEL), SG_WIDTH ** -0.5),
        "w_out": nrm(ks[16], (DEPTH, D_MODEL, D_MODEL), D_MODEL ** -0.5),
        "norm2_g": 1.0 + nrm(ks[17], (DEPTH, D_MODEL), 0.02),
        "w_ffn_in": nrm(ks[18], (DEPTH, D_MODEL, 2 * FFN_HIDDEN), D_MODEL ** -0.5),
        "w_ffn_out": nrm(ks[19], (DEPTH, FFN_HIDDEN, D_MODEL), FFN_HIDDEN ** -0.5),
        "final_norm_g": 1.0 + nrm(ks[20], (D_MODEL,), 0.02),
    }


def reference(x, c, ctx, c_ctx, w_mod, b_mod, norm1_g, w_in, ret_decay_fwd, ret_decay_bwd,
              sg_ln_g, sg_ln_b, sg_w, sg_b, w_ret_o, w_sg_o, w_out, norm2_g,
              w_ffn_in, w_ffn_out, final_norm_g):
    B, L, _ = x.shape
    rope = rope_tables(L)
    ctx_s = ctx
    for l in range(DEPTH):
        sh1, sc1, gt1, sh2, sc2, gt2 = (t[:, None, :] for t in adaln(c, w_mod[l], b_mod[l]))
        csh1, csc1, cgt1, csh2, csc2, cgt2 = adaln(c_ctx, w_mod[l], b_mod[l])
        lg_f = log_decay(ret_decay_fwd[l])
        lg_b = log_decay(ret_decay_bwd[l])
        mixer_params = (w_in[l], lg_f, lg_b, sg_ln_g[l], sg_ln_b[l], sg_w[l], sg_b[l],
                        w_ret_o[l], w_sg_o[l], w_out[l])

        hc = modulate(rmsnorm(ctx_s, norm1_g[l]), csh1, csc1)
        Lc = hc.shape[1]
        kc = (hc @ w_in[l][:, K_OFF:V_OFF]).reshape(B, Lc, RET_HEADS, RET_DK) * (RET_DK ** -0.5)
        vc = (hc @ w_in[l][:, V_OFF:G_OFF]).reshape(B, Lc, RET_HEADS, RET_DV)
        s_f, s_b = context_states(jnp.swapaxes(kc.astype(F32), 1, 2),
                                  jnp.swapaxes(vc.astype(F32), 1, 2), lg_f, lg_b)

        h = modulate(rmsnorm(x, norm1_g[l]), sh1, sc1)
        x = x + gt1 * token_mixer(h, s_f, s_b, rope, *mixer_params)
        h2 = modulate(rmsnorm(x, norm2_g[l]), sh2, sc2)
        x = x + gt2 * swiglu(h2, w_ffn_in[l], w_ffn_out[l])

        if l < DEPTH - 1:
            zero = jnp.zeros((B, RET_HEADS, RET_DK, RET_DV), F32)
            ctx_s = ctx_s + cgt1 * token_mixer(hc, zero, zero, None, *mixer_params)
            hc2 = modulate(rmsnorm(ctx_s, norm2_g[l]), csh2, csc2)
            ctx_s = ctx_s + cgt2 * swiglu(hc2, w_ffn_in[l], w_ffn_out[l])
    return rmsnorm(x, final_norm_g)
```

```python
import functools
import math

import jax
import jax.numpy as jnp
from jax import lax
from jax.experimental import pallas as pl
from jax.experimental.pallas import tpu as pltpu

F32 = jnp.float32
BF16 = jnp.bfloat16

EPS = 1e-6
GRID_W = 64
ROPE_BASE = 10000.0
RET_HEADS = 8
SG_GROUPS = 8
SG_CHUNK = 128
V7X_VMEM_BYTES = 64 * 1024 * 1024
VMEM_LIMIT = V7X_VMEM_BYTES - 8 * 1024 * 1024
RET_C = 256


def _params(*sem):
    return pltpu.CompilerParams(dimension_semantics=sem, vmem_limit_bytes=VMEM_LIMIT)


def _sigmoid(x):
    return 1.0 / (1.0 + jnp.exp(-x))


def _silu(x):
    return x * _sigmoid(x)


def _gelu_tanh(x):
    c = math.sqrt(2.0 / math.pi)
    return x * (0.5 * (1.0 + jnp.tanh(c * (x + 0.044715 * (x * x * x)))))


def _rms(x):
    return x * lax.rsqrt(jnp.mean(x * x, axis=-1, keepdims=True) + EPS)


def _adaln_kernel(c_ref, w_ref, b_ref, o_ref):
    s = _silu(c_ref[...]).astype(BF16)
    o_ref[...] = jnp.dot(s, w_ref[...].astype(BF16), preferred_element_type=F32) + b_ref[...]


def _adaln(cond, w_mod, b_mod, tn=1024):
    rows, d = cond.shape
    n = w_mod.shape[1]
    return pl.pallas_call(
        _adaln_kernel,
        grid=(n // tn,),
        in_specs=[
            pl.BlockSpec((rows, d), lambda j: (0, 0)),
            pl.BlockSpec((d, tn), lambda j: (0, j)),
            pl.BlockSpec((1, tn), lambda j: (0, j)),
        ],
        out_specs=pl.BlockSpec((rows, tn), lambda j: (0, j)),
        out_shape=jax.ShapeDtypeStruct((rows, n), F32),
        compiler_params=_params("arbitrary"),
        name="adaln",
    )(cond, w_mod, b_mod.reshape(1, n))


def _prenorm_kernel(x_ref, g_ref, sh_ref, sc_ref, o_ref):
    y = _rms(x_ref[...]) * g_ref[...]
    o_ref[...] = (y * (1.0 + sc_ref[0]) + sh_ref[0]).astype(BF16)


def _prenorm(x2d, g, mod3, rows_per_batch, mod_row, shift_blk, scale_blk, tl):
    m, d = x2d.shape
    per = rows_per_batch // tl
    return pl.pallas_call(
        _prenorm_kernel,
        grid=(m // tl,),
        in_specs=[
            pl.BlockSpec((tl, d), lambda i: (i, 0)),
            pl.BlockSpec((1, d), lambda i: (0, 0)),
            pl.BlockSpec((1, 1, d), lambda i: (mod_row(i // per), 0, shift_blk)),
            pl.BlockSpec((1, 1, d), lambda i: (mod_row(i // per), 0, scale_blk)),
        ],
        out_specs=pl.BlockSpec((tl, d), lambda i: (i, 0)),
        out_shape=jax.ShapeDtypeStruct((m, d), BF16),
        compiler_params=_params("arbitrary"),
        name="prenorm",
    )(x2d, g.reshape(1, d), mod3, mod3)


SUB_N = 256


def _proj_kernel(h_ref, w_ref, *rest, act, col_scale, tn):
    if act == "rope":
        cos_ref, sin_ref, o_ref = rest
    else:
        (o_ref,) = rest
    h = h_ref[...]
    if col_scale is not None:
        lo, hi, scale = col_scale
        j = pl.program_id(1)
        mult = jnp.where(jnp.logical_and(j >= lo, j < hi), scale, 1.0).astype(F32)
    half = SUB_N // 2
    for s in range(tn // SUB_N):
        acc = jnp.dot(h, w_ref[:, s * SUB_N:(s + 1) * SUB_N], preferred_element_type=F32)
        if col_scale is not None:
            acc = acc * mult
        if act == "silu":
            acc = _silu(acc)
        elif act == "gelu":
            acc = _gelu_tanh(acc)
        elif act == "sigmoid":
            acc = _sigmoid(acc)
        if act == "rope":
            for a in range(2):
                xa = acc[:, a * half:(a + 1) * half]
                ya = (xa * cos_ref[:, a * half:(a + 1) * half]
                      + pltpu.roll(xa, half // 2, axis=1) * sin_ref[:, a * half:(a + 1) * half])
                o_ref[:, s * SUB_N + a * half:s * SUB_N + (a + 1) * half] = ya.astype(BF16)
        else:
            o_ref[:, s * SUB_N:(s + 1) * SUB_N] = acc.astype(BF16)


def _proj(h, w, col0, ncols, act, *, tm, tn, rope=None, seq_len=None, col_scale=None,
          name="proj"):
    m, k = h.shape
    blk0 = col0 // tn
    in_specs = [
        pl.BlockSpec((tm, k), lambda i, j: (i, 0)),
        pl.BlockSpec((k, tn), lambda i, j: (0, blk0 + j)),
    ]
    args = [h, w]
    if act == "rope":
        per = seq_len // tm
        spec = pl.BlockSpec((tm, SUB_N), lambda i, j: (i % per, 0))
        in_specs += [spec, spec]
        args += list(rope)
    return pl.pallas_call(
        functools.partial(_proj_kernel, act=act, col_scale=col_scale, tn=tn),
        grid=(m // tm, ncols // tn),
        in_specs=in_specs,
        out_specs=pl.BlockSpec((tm, tn), lambda i, j: (i, j)),
        out_shape=jax.ShapeDtypeStruct((m, ncols), BF16),
        compiler_params=_params("arbitrary", "arbitrary"),
        name=name,
    )(*args)


def _dot_t(a, b):
    return lax.dot_general(a, b, (((0,), (0,)), ((), ())), preferred_element_type=F32)


def _dot_nt(a, b):
    return lax.dot_general(a, b, (((1,), (1,)), ((), ())), preferred_element_type=F32)


def _ret_kernel(lg_ref, q_ref, k_ref, v_ref, g_ref, kc_ref, vc_ref, o_ref, sf_ref, sb_ref,
                *, n_chunks):
    c = RET_C
    hd = pl.program_id(1)
    lgf = lg_ref[0, hd]
    lgb = lg_ref[1, hd]
    ri = lax.broadcasted_iota(jnp.int32, (c, c), 0).astype(F32)
    ci = lax.broadcasted_iota(jnp.int32, (c, c), 1).astype(F32)
    diff = ri - ci
    dmask = jnp.where(diff >= 0.0,
                      jnp.exp(lgf * jnp.maximum(diff, 0.0)),
                      jnp.exp(lgb * jnp.maximum(-diff, 0.0)))
    qdf = jnp.exp(lgf * (ri + 1.0))
    qdb = jnp.exp(lgb * (c - ri))
    kdf = jnp.exp(lgf * (c - 1.0 - ri))
    kdb = jnp.exp(lgb * ri)
    zero = jnp.zeros((1, c), F32)
    cdf = jnp.exp(zero + lgf * c)
    cdb = jnp.exp(zero + lgb * c)

    def kv(kn, vn, dec):
        return _dot_t((kn.astype(F32) * dec).astype(BF16), vn)

    kc = kc_ref[...]
    vc = vc_ref[...]
    sf_ref[...] = kv(kc, vc, kdf)
    sb = kv(kc, vc, kdb)
    for n in reversed(range(n_chunks)):
        rows = slice(n * c, (n + 1) * c)
        sb_ref[n] = sb.astype(BF16)
        if n > 0:
            sb = sb * cdb + kv(k_ref[rows, :], v_ref[rows, :], kdb)
    for n in range(n_chunks):
        rows = slice(n * c, (n + 1) * c)
        qn = q_ref[rows, :]
        kn = k_ref[rows, :]
        vn = v_ref[rows, :]
        a = (_dot_nt(qn, kn) * dmask).astype(BF16)
        o = (jnp.dot(a, vn, preferred_element_type=F32)
             + jnp.dot(qn, sf_ref[...].astype(BF16), preferred_element_type=F32) * qdf
             + jnp.dot(qn, sb_ref[n], preferred_element_type=F32) * qdb)
        o = _rms(o)
        o_ref[rows, :] = (o * g_ref[rows, :].astype(F32)).astype(BF16)
        if n + 1 < n_chunks:
            sf_ref[...] = sf_ref[...] * cdf + kv(kn, vn, kdf)


def _retention(lg, qk, v, g, kvc, batch, seq_len, ctx_len):
    m = qk.shape[0]
    hd = RET_HEADS
    dk = qk.shape[1] // (2 * hd)
    dv = v.shape[1] // hd
    assert dk == RET_C and dv == RET_C and ctx_len == RET_C and seq_len % RET_C == 0
    n_chunks = seq_len // RET_C
    return pl.pallas_call(
        functools.partial(_ret_kernel, n_chunks=n_chunks),
        grid=(batch, hd),
        in_specs=[
            pl.BlockSpec(memory_space=pltpu.SMEM),
            pl.BlockSpec((seq_len, dk), lambda b, h: (b, h)),
            pl.BlockSpec((seq_len, dk), lambda b, h: (b, hd + h)),
            pl.BlockSpec((seq_len, dv), lambda b, h: (b, h)),
            pl.BlockSpec((seq_len, dv), lambda b, h: (b, h)),
            pl.BlockSpec((ctx_len, dk), lambda b, h: (b, h)),
            pl.BlockSpec((ctx_len, dv), lambda b, h: (b, hd + h)),
        ],
        out_specs=pl.BlockSpec((seq_len, dv), lambda b, h: (b, h)),
        out_shape=jax.ShapeDtypeStruct((m, hd * dv), BF16),
        scratch_shapes=[
            pltpu.VMEM((dk, dv), F32),
            pltpu.VMEM((n_chunks, dk, dv), BF16),
        ],
        compiler_params=_params("arbitrary", "arbitrary"),
        name="retention",
    )(lg, qk, qk, v, g, kvc, kvc)


def _mixout_kernel(ret_ref, u_ref, vs_ref, gr_ref, gs_ref, x_ref, lng_ref, lnb_ref, sgw_ref,
                   sgb_ref, wr_ref, ws_ref, wo_ref, gt1_ref, g2_ref, sh2_ref, sc2_ref,
                   x1_ref, h2_ref, sgo_ref, *, tm):
    vs = vs_ref[...].astype(F32)
    mu = jnp.mean(vs, axis=-1, keepdims=True)
    cen = vs - mu
    var = jnp.mean(cen * cen, axis=-1, keepdims=True)
    vn = ((cen * lax.rsqrt(var + EPS)) * lng_ref[...] + lnb_ref[...]).astype(BF16)
    gd = vn.shape[1] // SG_GROUPS
    for cidx in range(tm // SG_CHUNK):
        rows = slice(cidx * SG_CHUNK, (cidx + 1) * SG_CHUNK)
        for gi in range(SG_GROUPS):
            cols = slice(gi * gd, (gi + 1) * gd)
            mixed = jnp.dot(sgw_ref[gi], vn[rows, cols], preferred_element_type=F32)
            mixed = mixed + sgb_ref[:, cols]
            sgo_ref[rows, cols] = (u_ref[rows, cols].astype(F32) * mixed).astype(BF16)
    yr = jnp.dot(ret_ref[...], wr_ref[...], preferred_element_type=F32)
    ys = jnp.dot(sgo_ref[...], ws_ref[...], preferred_element_type=F32)
    merged = (gr_ref[...].astype(F32) * yr + gs_ref[...].astype(F32) * ys).astype(BF16)
    z = jnp.dot(merged, wo_ref[...], preferred_element_type=F32)
    x1 = x_ref[...] + gt1_ref[0] * z
    x1_ref[...] = x1
    y = _rms(x1) * g2_ref[...]
    h2_ref[...] = (y * (1.0 + sc2_ref[0]) + sh2_ref[0]).astype(BF16)


def _mixout(retg, uvs, gates, x2d, ln_g, ln_b, sgw, sgb_tab, wr, ws, wo, mod3, g2, seq_len, tm):
    m, d = x2d.shape
    w = retg.shape[1]
    per = seq_len // tm
    row = lambda i: (i, 0)
    row1 = lambda i: (i, 1)
    const2 = lambda i: (0, 0)
    resident = functools.partial(pl.BlockSpec, pipeline_mode=pl.Buffered(1))
    modspec = lambda blk: pl.BlockSpec((1, 1, d), lambda i: (i // per, 0, blk))
    return pl.pallas_call(
        functools.partial(_mixout_kernel, tm=tm),
        grid=(m // tm,),
        in_specs=[
            pl.BlockSpec((tm, w), row),
            pl.BlockSpec((tm, w), row),
            pl.BlockSpec((tm, w), row1),
            pl.BlockSpec((tm, d), row),
            pl.BlockSpec((tm, d), row1),
            pl.BlockSpec((tm, d), row),
            resident((1, w), const2),
            resident((1, w), const2),
            resident(sgw.shape, lambda i: (0, 0, 0)),
            resident(sgb_tab.shape, const2),
            resident(wr.shape, const2),
            resident(ws.shape, const2),
            resident(wo.shape, const2),
            modspec(2), pl.BlockSpec((1, d), const2), modspec(3), modspec(4),
        ],
        out_specs=[pl.BlockSpec((tm, d), row), pl.BlockSpec((tm, d), row)],
        out_shape=[jax.ShapeDtypeStruct((m, d), F32), jax.ShapeDtypeStruct((m, d), BF16)],
        scratch_shapes=[pltpu.VMEM((tm, w), BF16)],
        compiler_params=_params("arbitrary"),
        name="mixout",
    )(retg, uvs, uvs, gates, gates, x2d, ln_g.reshape(1, w), ln_b.reshape(1, w), sgw, sgb_tab,
      wr, ws, wo, mod3, g2.reshape(1, d), mod3, mod3)


def _ffn_kernel(h_ref, wa_ref, wb_ref, wo_ref, x1_ref, gt2_ref, gf_ref, o_ref, acc_ref):
    t = pl.program_id(1)
    h = h_ref[...]
    a = jnp.dot(h, wa_ref[...], preferred_element_type=F32)
    b = jnp.dot(h, wb_ref[...], preferred_element_type=F32)
    act = (_silu(a) * b).astype(BF16)
    part = jnp.dot(act, wo_ref[...], preferred_element_type=F32)

    @pl.when(t == 0)
    def _():
        acc_ref[...] = part

    @pl.when(t > 0)
    def _():
        acc_ref[...] += part

    @pl.when(t == pl.num_programs(1) - 1)
    def _():
        x2 = x1_ref[...] + gt2_ref[0] * acc_ref[...]
        o_ref[...] = _rms(x2) * gf_ref[...]


def _ffn(h2, w_in, w_out, x1, mod3, gf, seq_len, tm, th):
    m, d = x1.shape
    hidden = w_out.shape[0]
    nt = hidden // th
    per = seq_len // tm
    return pl.pallas_call(
        _ffn_kernel,
        grid=(m // tm, nt),
        in_specs=[
            pl.BlockSpec((tm, d), lambda i, t: (i, 0)),
            pl.BlockSpec((d, th), lambda i, t: (0, t)),
            pl.BlockSpec((d, th), lambda i, t: (0, nt + t)),
            pl.BlockSpec((th, d), lambda i, t: (t, 0)),
            pl.BlockSpec((tm, d), lambda i, t: (i, 0)),
            pl.BlockSpec((1, 1, d), lambda i, t: (i // per, 0, 5)),
            pl.BlockSpec((1, d), lambda i, t: (0, 0)),
        ],
        out_specs=pl.BlockSpec((tm, d), lambda i, t: (i, 0)),
        out_shape=jax.ShapeDtypeStruct((m, d), F32),
        scratch_shapes=[pltpu.VMEM((tm, d), F32)],
        compiler_params=_params("arbitrary", "arbitrary"),
        name="ffn",
    )(h2, w_in, w_in, w_out, x1, mod3, gf.reshape(1, d))


def _rope_tables(seq_len, dk):
    freqs = dk // 4
    rows = seq_len // GRID_W
    row = jnp.repeat(jnp.arange(rows), GRID_W)
    col = jnp.tile(jnp.arange(GRID_W), rows)
    freq = ROPE_BASE ** (-jnp.arange(freqs, dtype=F32) / freqs)
    ang = jnp.stack([row, col], axis=-1).astype(F32)[:, :, None] * freq
    cos, sin = jnp.cos(ang), jnp.sin(ang)
    cos_t = jnp.concatenate([cos[:, 0], cos[:, 0], cos[:, 1], cos[:, 1]], axis=-1)
    sin_t = jnp.concatenate([-sin[:, 0], sin[:, 0], -sin[:, 1], sin[:, 1]], axis=-1)
    return cos_t, sin_t


def kernel(x, c, ctx, c_ctx, w_mod, b_mod, norm1_g, w_in, ret_decay_fwd, ret_decay_bwd,
           sg_ln_g, sg_ln_b, sg_w, sg_b, w_ret_o, w_sg_o, w_out, norm2_g, w_ffn_in, w_ffn_out,
           final_norm_g):
    batch, seq_len, d = x.shape
    ctx_len = ctx.shape[1]
    depth = w_mod.shape[0]
    assert depth == 1
    width = w_ret_o.shape[1]
    dk = width // RET_HEADS
    q_off, k_off, v_off, g_off, u_off, gr_off = (i * width for i in (0, 1, 2, 3, 4, 6))

    x2d = x.reshape(batch * seq_len, d)
    ctx2d = ctx.reshape(batch * ctx_len, d)

    pad = (-(batch + 1)) % 8
    cond = jnp.concatenate([c, c_ctx[None], jnp.zeros((pad, d), F32)], axis=0)
    mod = _adaln(cond, w_mod[0], b_mod[0])
    mod3 = mod.reshape(mod.shape[0], 1, mod.shape[1])

    w_in_b = w_in[0].astype(BF16)
    lg = jnp.stack([-jax.nn.softplus(-ret_decay_fwd[0].astype(F32)),
                    -jax.nn.softplus(-ret_decay_bwd[0].astype(F32))])

    hc = _prenorm(ctx2d, norm1_g[0], mod3, ctx_len, lambda b: batch, 0, 1, tl=ctx_len)
    kvc = _proj(hc, w_in_b, k_off, 2 * width, "none", tm=hc.shape[0], tn=1024,
                col_scale=(0, width // 1024, dk ** -0.5), name="proj_ctx")

    h = _prenorm(x2d, norm1_g[0], mod3, seq_len, lambda b: b, 0, 1, tl=512)
    rope = _rope_tables(seq_len, dk)
    tm, tn = 1024, 1024
    qk = _proj(h, w_in_b, q_off, 2 * width, "rope", tm=tm, tn=tn, rope=rope, seq_len=seq_len,
               col_scale=(width // tn, 2 * width // tn, dk ** -0.5), name="proj_qk")
    v = _proj(h, w_in_b, v_off, width, "none", tm=tm, tn=tn, name="proj_v")
    g = _proj(h, w_in_b, g_off, width, "silu", tm=tm, tn=tn, name="proj_g")
    uvs = _proj(h, w_in_b, u_off, 2 * width, "gelu", tm=tm, tn=tn, name="proj_uvs")
    gates = _proj(h, w_in_b, gr_off, 2 * d, "sigmoid", tm=tm, tn=tn, name="proj_gates")

    retg = _retention(lg, qk, v, g, kvc, batch, seq_len, ctx_len)

    sgb_tab = jnp.repeat(sg_b[0].T, width // SG_GROUPS, axis=1)
    x1, h2 = _mixout(retg, uvs, gates, x2d, sg_ln_g[0], sg_ln_b[0], sg_w[0].astype(BF16), sgb_tab,
                     w_ret_o[0].astype(BF16), w_sg_o[0].astype(BF16), w_out[0].astype(BF16),
                     mod3, norm2_g[0], seq_len, tm=256)

    out = _ffn(h2, w_ffn_in[0].astype(BF16), w_ffn_out[0].astype(BF16), x1, mod3, final_norm_g,
               seq_len, tm=512, th=512)
    return out.reshape(batch, seq_len, d)
```

```python
import functools
import math

import jax
import jax.numpy as jnp
from jax import lax
from jax.experimental import pallas as pl
from jax.experimental.pallas import tpu as pltpu

F32 = jnp.float32
BF16 = jnp.bfloat16

EPS = 1e-6
GRID_W = 64
ROPE_BASE = 10000.0
RET_HEADS = 8
SG_GROUPS = 8
SG_CHUNK = 128
V7X_VMEM_BYTES = 64 * 1024 * 1024
VMEM_LIMIT = V7X_VMEM_BYTES - 8 * 1024 * 1024
MXU_N = 256
RET_C = MXU_N
SUB_ROWS = 256


def _params(*sem):
    return pltpu.CompilerParams(dimension_semantics=sem, vmem_limit_bytes=VMEM_LIMIT)


def _sigmoid(x):
    return 1.0 / (1.0 + jnp.exp(-x))


def _silu(x):
    return x * _sigmoid(x)


def _gelu_tanh(x):
    c = math.sqrt(2.0 / math.pi)
    return x * (0.5 * (1.0 + jnp.tanh(c * (x + 0.044715 * (x * x * x)))))


def _rms(x):
    return x * lax.rsqrt(jnp.mean(x * x, axis=-1, keepdims=True) + EPS)


def _dot(a, b):
    return jnp.dot(a, b, preferred_element_type=F32)


def _resident(shape, index_map):
    return pl.BlockSpec(shape, index_map, pipeline_mode=pl.Buffered(1))


def _adaln_kernel(c_ref, w_ref, b_ref, o_ref):
    s = _silu(c_ref[...]).astype(BF16)
    o_ref[...] = _dot(s, w_ref[...].astype(BF16)) + b_ref[...]


def _adaln(cond, w_mod, b_mod, tn=1024):
    rows, d = cond.shape
    n = w_mod.shape[1]
    return pl.pallas_call(
        _adaln_kernel,
        grid=(n // tn,),
        in_specs=[
            pl.BlockSpec((rows, d), lambda j: (0, 0)),
            pl.BlockSpec((d, tn), lambda j: (0, j)),
            pl.BlockSpec((1, tn), lambda j: (0, j)),
        ],
        out_specs=pl.BlockSpec((rows, tn), lambda j: (0, j)),
        out_shape=jax.ShapeDtypeStruct((rows, n), F32),
        compiler_params=_params("arbitrary"),
        name="adaln",
    )(cond, w_mod, b_mod.reshape(1, n))


def _modulated_norm(x, g, sh, sc):
    return ((_rms(x) * g) * (1.0 + sc) + sh).astype(BF16)


def _prenorm_kernel(x_ref, g_ref, sh_ref, sc_ref, o_ref):
    o_ref[...] = _modulated_norm(x_ref[...], g_ref[...], sh_ref[0], sc_ref[0])


def _prenorm(x2d, g, mod3, rows_per_batch, mod_row, shift_blk, scale_blk, tl):
    m, d = x2d.shape
    per = rows_per_batch // tl
    return pl.pallas_call(
        _prenorm_kernel,
        grid=(m // tl,),
        in_specs=[
            pl.BlockSpec((tl, d), lambda i: (i, 0)),
            pl.BlockSpec((1, d), lambda i: (0, 0)),
            pl.BlockSpec((1, 1, d), lambda i: (mod_row(i // per), 0, shift_blk)),
            pl.BlockSpec((1, 1, d), lambda i: (mod_row(i // per), 0, scale_blk)),
        ],
        out_specs=pl.BlockSpec((tl, d), lambda i: (i, 0)),
        out_shape=jax.ShapeDtypeStruct((m, d), BF16),
        compiler_params=_params("arbitrary"),
        name="prenorm",
    )(x2d, g.reshape(1, d), mod3, mod3)


def _proj_kernel(*refs, act, col_scale, tm, tn, fused_norm):
    refs = list(refs)
    if fused_norm:
        x_ref, g_ref, sh_ref, sc_ref = refs[:4]
        refs = refs[4:]
        h_ref = refs.pop()

        @pl.when(pl.program_id(1) == 0)
        def _():
            def body(r, carry):
                rows = pl.ds(pl.multiple_of(r * SUB_ROWS, SUB_ROWS), SUB_ROWS)
                h_ref[rows, :] = _modulated_norm(x_ref[rows, :], g_ref[...], sh_ref[0], sc_ref[0])
                return carry
            lax.fori_loop(0, tm // SUB_ROWS, body, 0)
    else:
        h_ref = refs.pop(0)
    w_ref = refs.pop(0)
    o_ref = refs.pop()
    if act == "rope":
        cos_ref, sin_ref = refs
    if col_scale is not None:
        lo, hi, scale = col_scale
        j = pl.program_id(1)
        mult = jnp.where(jnp.logical_and(j >= lo, j < hi), scale, 1.0).astype(F32)
    half = MXU_N // 2
    for r0 in range(0, tm, SUB_ROWS):
        rows = slice(r0, r0 + SUB_ROWS)
        acc = _dot(h_ref[rows, :], w_ref[...])
        if col_scale is not None:
            acc = acc * mult
        if act == "silu":
            acc = _silu(acc)
        elif act == "gelu":
            acc = _gelu_tanh(acc)
        elif act == "sigmoid":
            acc = _sigmoid(acc)
        if act == "rope":
            for s in range(tn // half):
                lanes = slice((s % 2) * half, (s % 2 + 1) * half)
                xa = acc[:, s * half:(s + 1) * half]
                ya = (xa * cos_ref[rows, lanes]
                      + pltpu.roll(xa, half // 2, axis=1) * sin_ref[rows, lanes])
                o_ref[rows, s * half:(s + 1) * half] = ya.astype(BF16)
        else:
            o_ref[rows, :] = acc.astype(BF16)


def _proj(h, w, col0, ncols, act, *, tm, tn, rope=None, seq_len=None, col_scale=None,
          norm=None, name="proj"):
    m, k = h.shape
    blk0 = col0 // tn
    in_specs, args = [], []
    if norm is not None:
        g, mod3, shift_blk, scale_blk = norm
        per = seq_len // tm
        in_specs += [
            pl.BlockSpec((tm, k), lambda i, j: (i, 0)),
            pl.BlockSpec((1, k), lambda i, j: (0, 0)),
            pl.BlockSpec((1, 1, k), lambda i, j: (i // per, 0, shift_blk)),
            pl.BlockSpec((1, 1, k), lambda i, j: (i // per, 0, scale_blk)),
        ]
        args += [h, g.reshape(1, k), mod3, mod3]
    else:
        in_specs.append(pl.BlockSpec((tm, k), lambda i, j: (i, 0)))
        args.append(h)
    in_specs.append(pl.BlockSpec((k, tn), lambda i, j: (0, blk0 + j)))
    args.append(w)
    if act == "rope":
        per_l = seq_len // tm
        spec = pl.BlockSpec((tm, MXU_N), lambda i, j: (i % per_l, 0))
        in_specs += [spec, spec]
        args += list(rope)
    out_specs = pl.BlockSpec((tm, tn), lambda i, j: (i, j))
    out_shape = jax.ShapeDtypeStruct((m, ncols), BF16)
    if norm is not None:
        out_specs = [out_specs, pl.BlockSpec((tm, k), lambda i, j: (i, 0))]
        out_shape = [out_shape, jax.ShapeDtypeStruct((m, k), BF16)]
    return pl.pallas_call(
        functools.partial(_proj_kernel, act=act, col_scale=col_scale, tm=tm, tn=tn,
                          fused_norm=norm is not None),
        grid=(m // tm, ncols // tn),
        in_specs=in_specs,
        out_specs=out_specs,
        out_shape=out_shape,
        compiler_params=_params("arbitrary", "arbitrary"),
        name=name,
    )(*args)


def _dot_t(a, b):
    return lax.dot_general(a, b, (((0,), (0,)), ((), ())), preferred_element_type=F32)


def _dot_nt(a, b):
    return lax.dot_general(a, b, (((1,), (1,)), ((), ())), preferred_element_type=F32)


def _ret_kernel(lg_ref, q_ref, k_ref, v_ref, g_ref, kc_ref, vc_ref, o_ref, sf_ref, sb_ref,
                *, n_chunks):
    c = RET_C
    hd = pl.program_id(1)
    lgf = lg_ref[0, hd]
    lgb = lg_ref[1, hd]
    ri = lax.broadcasted_iota(jnp.int32, (c, c), 0).astype(F32)
    ci = lax.broadcasted_iota(jnp.int32, (c, c), 1).astype(F32)
    diff = ri - ci
    dmask = jnp.where(diff >= 0.0,
                      jnp.exp(lgf * jnp.maximum(diff, 0.0)),
                      jnp.exp(lgb * jnp.maximum(-diff, 0.0)))
    qdf = jnp.exp(lgf * (ri + 1.0))
    qdb = jnp.exp(lgb * (c - ri))
    kdf = jnp.exp(lgf * (c - 1.0 - ri))
    kdb = jnp.exp(lgb * ri)
    zero = jnp.zeros((1, c), F32)
    cdf = jnp.exp(zero + lgf * c)
    cdb = jnp.exp(zero + lgb * c)

    def kv(kn, vn, dec):
        return _dot_t((kn.astype(F32) * dec).astype(BF16), vn)

    kc = kc_ref[...]
    vc = vc_ref[...]
    sf_ref[...] = kv(kc, vc, kdf)
    sb = kv(kc, vc, kdb)
    for n in reversed(range(n_chunks)):
        rows = slice(n * c, (n + 1) * c)
        sb_ref[n] = sb.astype(BF16)
        if n > 0:
            sb = sb * cdb + kv(k_ref[rows, :], v_ref[rows, :], kdb)
    for n in range(n_chunks):
        rows = slice(n * c, (n + 1) * c)
        qn = q_ref[rows, :]
        kn = k_ref[rows, :]
        vn = v_ref[rows, :]
        a = (_dot_nt(qn, kn) * dmask).astype(BF16)
        o = (_dot(a, vn)
             + _dot(qn, sf_ref[...].astype(BF16)) * qdf
             + _dot(qn, sb_ref[n]) * qdb)
        o = _rms(o)
        o_ref[rows, :] = (o * g_ref[rows, :].astype(F32)).astype(BF16)
        if n + 1 < n_chunks:
            sf_ref[...] = sf_ref[...] * cdf + kv(kn, vn, kdf)


def _retention(lg, qk, v, g, kvc, batch, seq_len, ctx_len):
    m = qk.shape[0]
    hd = RET_HEADS
    dk = qk.shape[1] // (2 * hd)
    dv = v.shape[1] // hd
    assert dk == RET_C and dv == RET_C and ctx_len == RET_C and seq_len % RET_C == 0
    n_chunks = seq_len // RET_C
    return pl.pallas_call(
        functools.partial(_ret_kernel, n_chunks=n_chunks),
        grid=(batch, hd),
        in_specs=[
            pl.BlockSpec(memory_space=pltpu.SMEM),
            pl.BlockSpec((seq_len, dk), lambda b, h: (b, h)),
            pl.BlockSpec((seq_len, dk), lambda b, h: (b, hd + h)),
            pl.BlockSpec((seq_len, dv), lambda b, h: (b, h)),
            pl.BlockSpec((seq_len, dv), lambda b, h: (b, h)),
            pl.BlockSpec((ctx_len, dk), lambda b, h: (b, h)),
            pl.BlockSpec((ctx_len, dv), lambda b, h: (b, hd + h)),
        ],
        out_specs=pl.BlockSpec((seq_len, dv), lambda b, h: (b, h)),
        out_shape=jax.ShapeDtypeStruct((m, hd * dv), BF16),
        scratch_shapes=[
            pltpu.VMEM((dk, dv), F32),
            pltpu.VMEM((n_chunks, dk, dv), BF16),
        ],
        compiler_params=_params("arbitrary", "arbitrary"),
        name="retention",
    )(lg, qk, qk, v, g, kvc, kvc)


def _branch_kernel(ret_ref, u_ref, vs_ref, gr_ref, gs_ref, lng_ref, lnb_ref, sgw_ref, sgb_ref,
                   wr_ref, ws_ref, o_ref, sgo_ref, *, tm):
    width = vs_ref.shape[1]
    gd = width // SG_GROUPS
    for r0 in range(0, tm, SUB_ROWS):
        rows = slice(r0, r0 + SUB_ROWS)
        vs = vs_ref[rows, :].astype(F32)
        mu = jnp.mean(vs, axis=-1, keepdims=True)
        cen = vs - mu
        var = jnp.mean(cen * cen, axis=-1, keepdims=True)
        vn = ((cen * lax.rsqrt(var + EPS)) * lng_ref[...] + lnb_ref[...]).astype(BF16)
        for c0 in range(0, SUB_ROWS, SG_CHUNK):
            crow = slice(r0 + c0, r0 + c0 + SG_CHUNK)
            for gi in range(SG_GROUPS):
                cols = slice(gi * gd, (gi + 1) * gd)
                mixed = _dot(sgw_ref[gi], vn[c0:c0 + SG_CHUNK, cols]) + sgb_ref[:, cols]
                sgo_ref[crow, cols] = (u_ref[crow, cols].astype(F32) * mixed).astype(BF16)
        ret = ret_ref[rows, :]
        sgo = sgo_ref[rows, :]
        for n0 in range(0, o_ref.shape[1], MXU_N):
            cols = slice(n0, n0 + MXU_N)
            yr = _dot(ret, wr_ref[:, cols])
            ys = _dot(sgo, ws_ref[:, cols])
            o_ref[rows, cols] = (gr_ref[rows, cols].astype(F32) * yr
                                 + gs_ref[rows, cols].astype(F32) * ys).astype(BF16)


def _branches(retg, uvs, gates, ln_g, ln_b, sgw, sgb_tab, wr, ws, tm):
    m, w = retg.shape
    d = wr.shape[1]
    row = lambda i: (i, 0)
    row1 = lambda i: (i, 1)
    const2 = lambda i: (0, 0)
    return pl.pallas_call(
        functools.partial(_branch_kernel, tm=tm),
        grid=(m // tm,),
        in_specs=[
            pl.BlockSpec((tm, w), row),
            pl.BlockSpec((tm, w), row),
            pl.BlockSpec((tm, w), row1),
            pl.BlockSpec((tm, d), row),
            pl.BlockSpec((tm, d), row1),
            _resident((1, w), const2),
            _resident((1, w), const2),
            _resident(sgw.shape, lambda i: (0, 0, 0)),
            _resident(sgb_tab.shape, const2),
            _resident(wr.shape, const2),
            _resident(ws.shape, const2),
        ],
        out_specs=pl.BlockSpec((tm, d), row),
        out_shape=jax.ShapeDtypeStruct((m, d), BF16),
        scratch_shapes=[pltpu.VMEM((tm, w), BF16)],
        compiler_params=_params("arbitrary"),
        name="branches",
    )(retg, uvs, uvs, gates, gates, ln_g.reshape(1, w), ln_b.reshape(1, w), sgw, sgb_tab, wr, ws)


def _outproj_kernel(m_ref, wo_ref, x_ref, gt1_ref, g2_ref, sh2_ref, sc2_ref, x1_ref, h2_ref, *, tm):
    for r0 in range(0, tm, SUB_ROWS):
        rows = slice(r0, r0 + SUB_ROWS)
        mr = m_ref[rows, :]
        for n0 in range(0, x1_ref.shape[1], MXU_N):
            cols = slice(n0, n0 + MXU_N)
            x1_ref[rows, cols] = x_ref[rows, cols] + gt1_ref[0, :, cols] * _dot(mr, wo_ref[:, cols])
        h2_ref[rows, :] = _modulated_norm(x1_ref[rows, :], g2_ref[...], sh2_ref[0], sc2_ref[0])


def _outproj(merged, wo, x2d, mod3, g2, seq_len, tm):
    m, d = x2d.shape
    per = seq_len // tm
    row = lambda i: (i, 0)
    modspec = lambda blk: pl.BlockSpec((1, 1, d), lambda i: (i // per, 0, blk))
    return pl.pallas_call(
        functools.partial(_outproj_kernel, tm=tm),
        grid=(m // tm,),
        in_specs=[
            pl.BlockSpec((tm, d), row),
            _resident(wo.shape, lambda i: (0, 0)),
            pl.BlockSpec((tm, d), row),
            modspec(2), pl.BlockSpec((1, d), lambda i: (0, 0)), modspec(3), modspec(4),
        ],
        out_specs=[pl.BlockSpec((tm, d), row), pl.BlockSpec((tm, d), row)],
        out_shape=[jax.ShapeDtypeStruct((m, d), F32), jax.ShapeDtypeStruct((m, d), BF16)],
        compiler_params=_params("arbitrary"),
        name="outproj",
    )(merged, wo, x2d, mod3, g2.reshape(1, d), mod3, mod3)


def _ffn_up_kernel(h_ref, wa_ref, wb_ref, o_ref, *, tm):
    for r0 in range(0, tm, SUB_ROWS):
        rows = slice(r0, r0 + SUB_ROWS)
        hr = h_ref[rows, :]
        a = _dot(hr, wa_ref[...])
        b = _dot(hr, wb_ref[...])
        o_ref[rows, :] = (_silu(a) * b).astype(BF16)


def _ffn_up(h2, w_in, tm, tn):
    m, d = h2.shape
    hidden = w_in.shape[1] // 2
    nt = hidden // tn
    return pl.pallas_call(
        functools.partial(_ffn_up_kernel, tm=tm),
        grid=(m // tm, nt),
        in_specs=[
            pl.BlockSpec((tm, d), lambda i, j: (i, 0)),
            pl.BlockSpec((d, tn), lambda i, j: (0, j)),
            pl.BlockSpec((d, tn), lambda i, j: (0, nt + j)),
        ],
        out_specs=pl.BlockSpec((tm, tn), lambda i, j: (i, j)),
        out_shape=jax.ShapeDtypeStruct((m, hidden), BF16),
        compiler_params=_params("arbitrary", "arbitrary"),
        name="ffn_up",
    )(h2, w_in, w_in)


def _ffn_down_kernel(a_ref, w_ref, x1_ref, gt2_ref, gf_ref, o_ref, *, tm):
    for r0 in range(0, tm, SUB_ROWS):
        rows = slice(r0, r0 + SUB_ROWS)
        ar = a_ref[rows, :]
        for n0 in range(0, o_ref.shape[1], MXU_N):
            cols = slice(n0, n0 + MXU_N)
            o_ref[rows, cols] = x1_ref[rows, cols] + gt2_ref[0, :, cols] * _dot(ar, w_ref[:, cols])
        o_ref[rows, :] = _rms(o_ref[rows, :]) * gf_ref[...]


def _ffn_down(act, w_out, x1, mod3, gf, seq_len, tm):
    m, d = x1.shape
    hidden = act.shape[1]
    per = seq_len // tm
    row = lambda i: (i, 0)
    return pl.pallas_call(
        functools.partial(_ffn_down_kernel, tm=tm),
        grid=(m // tm,),
        in_specs=[
            pl.BlockSpec((tm, hidden), row),
            _resident(w_out.shape, lambda i: (0, 0)),
            pl.BlockSpec((tm, d), row),
            pl.BlockSpec((1, 1, d), lambda i: (i // per, 0, 5)),
            pl.BlockSpec((1, d), lambda i: (0, 0)),
        ],
        out_specs=pl.BlockSpec((tm, d), row),
        out_shape=jax.ShapeDtypeStruct((m, d), F32),
        compiler_params=_params("arbitrary"),
        name="ffn_down",
    )(act, w_out, x1, mod3, gf.reshape(1, d))


def _rope_tables(seq_len, dk):
    freqs = dk // 4
    rows = seq_len // GRID_W
    row = jnp.repeat(jnp.arange(rows), GRID_W)
    col = jnp.tile(jnp.arange(GRID_W), rows)
    freq = ROPE_BASE ** (-jnp.arange(freqs, dtype=F32) / freqs)
    ang = jnp.stack([row, col], axis=-1).astype(F32)[:, :, None] * freq
    cos, sin = jnp.cos(ang), jnp.sin(ang)
    cos_t = jnp.concatenate([cos[:, 0], cos[:, 0], cos[:, 1], cos[:, 1]], axis=-1)
    sin_t = jnp.concatenate([-sin[:, 0], sin[:, 0], -sin[:, 1], sin[:, 1]], axis=-1)
    return cos_t, sin_t


def kernel(x, c, ctx, c_ctx, w_mod, b_mod, norm1_g, w_in, ret_decay_fwd, ret_decay_bwd,
           sg_ln_g, sg_ln_b, sg_w, sg_b, w_ret_o, w_sg_o, w_out, norm2_g, w_ffn_in, w_ffn_out,
           final_norm_g):
    batch, seq_len, d = x.shape
    ctx_len = ctx.shape[1]
    depth = w_mod.shape[0]
    assert depth == 1
    width = w_ret_o.shape[1]
    dk = width // RET_HEADS
    q_off, k_off, v_off, g_off, u_off, gr_off = (i * width for i in (0, 1, 2, 3, 4, 6))

    x2d = x.reshape(batch * seq_len, d)
    ctx2d = ctx.reshape(batch * ctx_len, d)

    pad = (-(batch + 1)) % 8
    cond = jnp.concatenate([c, c_ctx[None], jnp.zeros((pad, d), F32)], axis=0)
    mod = _adaln(cond, w_mod[0], b_mod[0])
    mod3 = mod.reshape(mod.shape[0], 1, mod.shape[1])

    w_in_b = w_in[0].astype(BF16)
    lg = jnp.stack([-jax.nn.softplus(-ret_decay_fwd[0].astype(F32)),
                    -jax.nn.softplus(-ret_decay_bwd[0].astype(F32))])

    hc = _prenorm(ctx2d, norm1_g[0], mod3, ctx_len, lambda b: batch, 0, 1, tl=ctx_len)
    kvc = _proj(hc, w_in_b, k_off, 2 * width, "none", tm=hc.shape[0], tn=1024,
                col_scale=(0, width // 1024, dk ** -0.5), name="proj_ctx")

    rope = _rope_tables(seq_len, dk)
    qk, h = _proj(x2d, w_in_b, q_off, 2 * width, "rope", tm=1024, tn=1024, rope=rope,
                  seq_len=seq_len, col_scale=(width // 1024, 2 * width // 1024, dk ** -0.5),
                  norm=(norm1_g[0], mod3, 0, 1), name="proj_qk")
    tm, tn = 2048, 1024
    v = _proj(h, w_in_b, v_off, width, "none", tm=tm, tn=tn, name="proj_v")
    g = _proj(h, w_in_b, g_off, width, "silu", tm=tm, tn=tn, name="proj_g")
    uvs = _proj(h, w_in_b, u_off, 2 * width, "gelu", tm=tm, tn=tn, name="proj_uvs")
    gates = _proj(h, w_in_b, gr_off, 2 * d, "sigmoid", tm=tm, tn=tn, name="proj_gates")

    retg = _retention(lg, qk, v, g, kvc, batch, seq_len, ctx_len)

    sgb_tab = jnp.repeat(sg_b[0].T, width // SG_GROUPS, axis=1)
    merged = _branches(retg, uvs, gates, sg_ln_g[0], sg_ln_b[0], sg_w[0].astype(BF16), sgb_tab,
                       w_ret_o[0].astype(BF16), w_sg_o[0].astype(BF16), tm=512)
    x1, h2 = _outproj(merged, w_out[0].astype(BF16), x2d, mod3, norm2_g[0], seq_len, tm=512)

    act = _ffn_up(h2, w_ffn_in[0].astype(BF16), tm=2048, tn=512)
    out = _ffn_down(act, w_ffn_out[0].astype(BF16), x1, mod3, final_norm_g, seq_len, tm=512)
    return out.reshape(batch, seq_len, d)
```

```python
import functools
import math

import jax
import jax.numpy as jnp
from jax import lax
from jax.experimental import pallas as pl
from jax.experimental.pallas import tpu as pltpu

F32 = jnp.float32
BF16 = jnp.bfloat16

EPS = 1e-6
GRID_W = 64
ROPE_BASE = 10000.0
RET_HEADS = 8
SG_GROUPS = 8
SG_CHUNK = 128
V7X_VMEM_BYTES = 64 * 1024 * 1024
VMEM_LIMIT = V7X_VMEM_BYTES - 8 * 1024 * 1024
MXU_N = 256
RET_C = MXU_N
SUB_ROWS = 256


def _params(*sem):
    return pltpu.CompilerParams(dimension_semantics=sem, vmem_limit_bytes=VMEM_LIMIT)


def _sigmoid(x):
    return 1.0 / (1.0 + jnp.exp(-x))


def _silu(x):
    return x * _sigmoid(x)


def _gelu_tanh(x):
    b = -2.0 * math.sqrt(2.0 / math.pi) * math.log2(math.e)
    a = b * 0.044715
    return x / (1.0 + jnp.exp2(x * (a * (x * x) + b)))


def _rms(x):
    return x * lax.rsqrt(jnp.mean(x * x, axis=-1, keepdims=True) + EPS)


def _dot(a, b):
    return jnp.dot(a, b, preferred_element_type=F32)


def _resident(shape, index_map):
    return pl.BlockSpec(shape, index_map, pipeline_mode=pl.Buffered(1))


def _adaln_kernel(c_ref, w_ref, b_ref, o_ref):
    s = _silu(c_ref[...]).astype(BF16)
    o_ref[...] = _dot(s, w_ref[...].astype(BF16)) + b_ref[...]


def _adaln(cond, w_mod, b_mod, tn=1024):
    rows, d = cond.shape
    n = w_mod.shape[1]
    return pl.pallas_call(
        _adaln_kernel,
        grid=(n // tn,),
        in_specs=[
            pl.BlockSpec((rows, d), lambda j: (0, 0)),
            pl.BlockSpec((d, tn), lambda j: (0, j)),
            pl.BlockSpec((1, tn), lambda j: (0, j)),
        ],
        out_specs=pl.BlockSpec((rows, tn), lambda j: (0, j)),
        out_shape=jax.ShapeDtypeStruct((rows, n), F32),
        compiler_params=_params("arbitrary"),
        name="adaln",
    )(cond, w_mod, b_mod.reshape(1, n))


def _modulated_norm(x, g, sh, sc):
    return ((_rms(x) * g) * (1.0 + sc) + sh).astype(BF16)


def _prenorm_kernel(x_ref, g_ref, sh_ref, sc_ref, o_ref):
    o_ref[...] = _modulated_norm(x_ref[...], g_ref[...], sh_ref[0], sc_ref[0])


def _prenorm(x2d, g, mod3, rows_per_batch, mod_row, shift_blk, scale_blk, tl):
    m, d = x2d.shape
    per = rows_per_batch // tl
    return pl.pallas_call(
        _prenorm_kernel,
        grid=(m // tl,),
        in_specs=[
            pl.BlockSpec((tl, d), lambda i: (i, 0)),
            pl.BlockSpec((1, d), lambda i: (0, 0)),
            pl.BlockSpec((1, 1, d), lambda i: (mod_row(i // per), 0, shift_blk)),
            pl.BlockSpec((1, 1, d), lambda i: (mod_row(i // per), 0, scale_blk)),
        ],
        out_specs=pl.BlockSpec((tl, d), lambda i: (i, 0)),
        out_shape=jax.ShapeDtypeStruct((m, d), BF16),
        compiler_params=_params("arbitrary"),
        name="prenorm",
    )(x2d, g.reshape(1, d), mod3, mod3)


def _activate(acc, act):
    if act == "silu":
        return _silu(acc)
    if act == "gelu":
        return _gelu_tanh(acc)
    if act == "sigmoid":
        return _sigmoid(acc)
    return acc


def _proj_kernel(*refs, act, col_scale, tm, tn, fused_norm, n_side):
    refs = list(refs)
    side_out = [refs.pop() for _ in range(n_side)][::-1]
    if fused_norm:
        x_ref, g_ref, sh_ref, sc_ref = refs[:4]
        refs = refs[4:]
        h_ref = refs.pop()
    else:
        h_ref = refs.pop(0)
    w_ref = refs.pop(0)
    o_ref = refs.pop()
    side_in = [refs.pop() for _ in range(n_side)][::-1]
    if act == "rope":
        cos_ref, sin_ref = refs
    for src, dst in zip(side_in, side_out):
        dst[...] = src[...].astype(BF16)
    if col_scale is not None:
        lo, hi, scale = col_scale
        j = pl.program_id(1)
        mult = jnp.where(jnp.logical_and(j >= lo, j < hi), scale, 1.0).astype(F32)
    half = MXU_N // 2

    def row_groups(with_norm):
        for r0 in range(0, tm, SUB_ROWS):
            rows = slice(r0, r0 + SUB_ROWS)
            if with_norm:
                h_ref[rows, :] = _modulated_norm(x_ref[rows, :], g_ref[...], sh_ref[0], sc_ref[0])
            acc = _dot(h_ref[rows, :], w_ref[...])
            if col_scale is not None:
                acc = acc * mult
            acc = _activate(acc, act)
            if act == "rope":
                for s in range(tn // half):
                    lanes = slice((s % 2) * half, (s % 2 + 1) * half)
                    xa = acc[:, s * half:(s + 1) * half]
                    ya = (xa * cos_ref[rows, lanes]
                          + pltpu.roll(xa, half // 2, axis=1) * sin_ref[rows, lanes])
                    o_ref[rows, s * half:(s + 1) * half] = ya.astype(BF16)
            else:
                o_ref[rows, :] = acc.astype(BF16)

    if fused_norm:
        pl.when(pl.program_id(1) == 0)(functools.partial(row_groups, True))
        pl.when(pl.program_id(1) > 0)(functools.partial(row_groups, False))
    else:
        row_groups(False)


def _proj(h, w, col0, ncols, act, *, tm, tn, rope=None, seq_len=None, col_scale=None,
          norm=None, sides=(), name="proj"):
    m, k = h.shape
    blk0 = col0 // tn
    nj = ncols // tn
    steps = (m // tm) * nj
    in_specs, args = [], []
    if norm is not None:
        g, mod3, shift_blk, scale_blk = norm
        per = seq_len // tm
        in_specs += [
            pl.BlockSpec((tm, k), lambda i, j: (i, 0)),
            pl.BlockSpec((1, k), lambda i, j: (0, 0)),
            pl.BlockSpec((1, 1, k), lambda i, j: (i // per, 0, shift_blk)),
            pl.BlockSpec((1, 1, k), lambda i, j: (i // per, 0, scale_blk)),
        ]
        args += [h, g.reshape(1, k), mod3, mod3]
    else:
        in_specs.append(pl.BlockSpec((tm, k), lambda i, j: (i, 0)))
        args.append(h)
    in_specs.append(pl.BlockSpec((k, tn), lambda i, j: (0, blk0 + j)))
    args.append(w)
    if act == "rope":
        per_l = seq_len // tm
        spec = pl.BlockSpec((tm, MXU_N), lambda i, j: (i % per_l, 0))
        in_specs += [spec, spec]
        args += list(rope)
    out_specs = [pl.BlockSpec((tm, tn), lambda i, j: (i, j))]
    out_shape = [jax.ShapeDtypeStruct((m, ncols), BF16)]
    if norm is not None:
        out_specs.append(pl.BlockSpec((tm, k), lambda i, j: (i, 0)))
        out_shape.append(jax.ShapeDtypeStruct((m, k), BF16))
    for side in sides:
        r, c = side.shape
        assert r % (steps * 16) == 0
        spec = pl.BlockSpec((r // steps, c), lambda i, j: (i * nj + j, 0))
        in_specs.append(spec)
        args.append(side)
        out_specs.append(spec)
        out_shape.append(jax.ShapeDtypeStruct((r, c), BF16))
    outs = pl.pallas_call(
        functools.partial(_proj_kernel, act=act, col_scale=col_scale, tm=tm, tn=tn,
                          fused_norm=norm is not None, n_side=len(sides)),
        grid=(m // tm, nj),
        in_specs=in_specs,
        out_specs=out_specs,
        out_shape=out_shape,
        compiler_params=_params("arbitrary", "arbitrary"),
        name=name,
    )(*args)
    return outs[0] if len(outs) == 1 else outs


def _castproj_kernel(hc_ref, w_ref, wb_ref, kv_ref, *, m, lo, hi, k_blocks, scale):
    j = pl.program_id(0)
    wb_ref[...] = w_ref[...].astype(BF16)

    @pl.when(jnp.logical_and(j >= lo, j < hi))
    def _():
        mult = jnp.where(j < lo + k_blocks, scale, 1.0).astype(F32)
        for r0 in range(0, m, SUB_ROWS):
            rows = slice(r0, r0 + SUB_ROWS)
            kv_ref[rows, :] = (_dot(hc_ref[rows, :], wb_ref[...]) * mult).astype(BF16)


def _castproj(hc, w_f32, k_col0, kv_cols, k_cols, scale, tn=1024):
    m, k = hc.shape
    n = w_f32.shape[1]
    lo, hi = k_col0 // tn, (k_col0 + kv_cols) // tn
    return pl.pallas_call(
        functools.partial(_castproj_kernel, m=m, lo=lo, hi=hi, k_blocks=k_cols // tn, scale=scale),
        grid=(n // tn,),
        in_specs=[
            _resident((m, k), lambda j: (0, 0)),
            pl.BlockSpec((k, tn), lambda j: (0, j)),
        ],
        out_specs=[
            pl.BlockSpec((k, tn), lambda j: (0, j)),
            pl.BlockSpec((m, tn), lambda j: (0, jnp.clip(j - lo, 0, hi - lo - 1))),
        ],
        out_shape=[jax.ShapeDtypeStruct((k, n), BF16), jax.ShapeDtypeStruct((m, kv_cols), BF16)],
        compiler_params=_params("arbitrary"),
        name="castproj_ctx",
    )(hc, w_f32)


def _dot_t(a, b):
    return lax.dot_general(a, b, (((0,), (0,)), ((), ())), preferred_element_type=F32)


def _dot_nt(a, b):
    return lax.dot_general(a, b, (((1,), (1,)), ((), ())), preferred_element_type=F32)


def _ret_kernel(lg_ref, q_ref, k_ref, v_ref, g_ref, kc_ref, vc_ref, o_ref, sf_ref, sb_ref,
                *, n_chunks, heads_per_step):
    c = RET_C
    ri = lax.broadcasted_iota(jnp.int32, (c, c), 0).astype(F32)
    ci = lax.broadcasted_iota(jnp.int32, (c, c), 1).astype(F32)
    diff = ri - ci
    zero = jnp.zeros((1, c), F32)

    def kv(kn, vn, dec):
        return _dot_t(kn * dec, vn)

    for t in range(heads_per_step):
        hd = pl.program_id(1) * heads_per_step + t
        hc = slice(t * c, (t + 1) * c)
        lgf = lg_ref[0, hd]
        lgb = lg_ref[1, hd]
        dmask = jnp.where(diff >= 0.0,
                          jnp.exp(lgf * jnp.maximum(diff, 0.0)),
                          jnp.exp(lgb * jnp.maximum(-diff, 0.0)))
        qdf = jnp.exp(lgf * (ri + 1.0))
        qdb = jnp.exp(lgb * (c - ri))
        kdf = jnp.exp(lgf * (c - 1.0 - ri)).astype(BF16)
        kdb = jnp.exp(lgb * ri).astype(BF16)
        cdf = jnp.exp(zero + lgf * c)
        cdb = jnp.exp(zero + lgb * c)

        kc = kc_ref[:, hc]
        vc = vc_ref[:, hc]
        sf_ref[t] = kv(kc, vc, kdf)
        sb = kv(kc, vc, kdb)
        for n in reversed(range(n_chunks)):
            rows = slice(n * c, (n + 1) * c)
            sb_ref[t, n] = sb.astype(BF16)
            if n > 0:
                sb = sb * cdb + kv(k_ref[rows, hc], v_ref[rows, hc], kdb)
        for n in range(n_chunks):
            rows = slice(n * c, (n + 1) * c)
            qn = q_ref[rows, hc]
            kn = k_ref[rows, hc]
            vn = v_ref[rows, hc]
            a = (_dot_nt(qn, kn) * dmask).astype(BF16)
            o = (_dot(a, vn)
                 + _dot(qn, sf_ref[t].astype(BF16)) * qdf
                 + _dot(qn, sb_ref[t, n]) * qdb)
            o_ref[rows, hc] = _rms(o).astype(BF16) * g_ref[rows, hc]
            if n + 1 < n_chunks:
                sf_ref[t] = sf_ref[t] * cdf + kv(kn, vn, kdf)


def _retention(lg, qk, v, g, kvc, batch, seq_len, ctx_len, heads_per_step=2):
    m = qk.shape[0]
    hd = RET_HEADS
    dk = qk.shape[1] // (2 * hd)
    dv = v.shape[1] // hd
    assert dk == RET_C and dv == RET_C and ctx_len == RET_C and seq_len % RET_C == 0
    n_chunks = seq_len // RET_C
    hps = heads_per_step
    hb = hd // hps
    return pl.pallas_call(
        functools.partial(_ret_kernel, n_chunks=n_chunks, heads_per_step=hps),
        grid=(batch, hb),
        in_specs=[
            pl.BlockSpec(memory_space=pltpu.SMEM),
            pl.BlockSpec((seq_len, hps * dk), lambda b, h: (b, h)),
            pl.BlockSpec((seq_len, hps * dk), lambda b, h: (b, hb + h)),
            pl.BlockSpec((seq_len, hps * dv), lambda b, h: (b, h)),
            pl.BlockSpec((seq_len, hps * dv), lambda b, h: (b, h)),
            pl.BlockSpec((ctx_len, hps * dk), lambda b, h: (b, h)),
            pl.BlockSpec((ctx_len, hps * dv), lambda b, h: (b, hb + h)),
        ],
        out_specs=pl.BlockSpec((seq_len, hps * dv), lambda b, h: (b, h)),
        out_shape=jax.ShapeDtypeStruct((m, hd * dv), BF16),
        scratch_shapes=[
            pltpu.VMEM((hps, dk, dv), F32),
            pltpu.VMEM((hps, n_chunks, dk, dv), BF16),
        ],
        compiler_params=_params("arbitrary", "arbitrary"),
        name="retention",
    )(lg, qk, qk, v, g, kvc, kvc)


def _branch_kernel(ret_ref, u_ref, vs_ref, gr_ref, gs_ref, lng_ref, lnb_ref, sgw_ref, sgb_ref,
                   wr_ref, ws_ref, o_ref, sgo_ref, *, tm):
    width = vs_ref.shape[1]
    gd = width // SG_GROUPS
    for r0 in range(0, tm, SUB_ROWS):
        rows = slice(r0, r0 + SUB_ROWS)
        vs = vs_ref[rows, :].astype(F32)
        mu = jnp.mean(vs, axis=-1, keepdims=True)
        cen = vs - mu
        var = jnp.mean(cen * cen, axis=-1, keepdims=True)
        vn = ((cen * lax.rsqrt(var + EPS)) * lng_ref[...] + lnb_ref[...]).astype(BF16)
        for c0 in range(0, SUB_ROWS, SG_CHUNK):
            crow = slice(r0 + c0, r0 + c0 + SG_CHUNK)
            for gi in range(SG_GROUPS):
                cols = slice(gi * gd, (gi + 1) * gd)
                mixed = _dot(sgw_ref[gi].astype(BF16), vn[c0:c0 + SG_CHUNK, cols]) + sgb_ref[:, cols]
                sgo_ref[crow, cols] = (u_ref[crow, cols].astype(F32) * mixed).astype(BF16)
        ret = ret_ref[rows, :]
        sgo = sgo_ref[rows, :]
        for n0 in range(0, o_ref.shape[1], MXU_N):
            cols = slice(n0, n0 + MXU_N)
            yr = _dot(ret, wr_ref[:, cols])
            ys = _dot(sgo, ws_ref[:, cols])
            o_ref[rows, cols] = (gr_ref[rows, cols].astype(F32) * yr
                                 + gs_ref[rows, cols].astype(F32) * ys).astype(BF16)


def _branches(retg, uvs, gates, ln_g, ln_b, sgw, sgb_tab, wr, ws, tm):
    m, w = retg.shape
    d = wr.shape[1]
    row = lambda i: (i, 0)
    row1 = lambda i: (i, 1)
    const2 = lambda i: (0, 0)
    return pl.pallas_call(
        functools.partial(_branch_kernel, tm=tm),
        grid=(m // tm,),
        in_specs=[
            pl.BlockSpec((tm, w), row),
            pl.BlockSpec((tm, w), row),
            pl.BlockSpec((tm, w), row1),
            pl.BlockSpec((tm, d), row),
            pl.BlockSpec((tm, d), row1),
            _resident((1, w), const2),
            _resident((1, w), const2),
            _resident(sgw.shape, lambda i: (0, 0, 0)),
            _resident(sgb_tab.shape, const2),
            _resident(wr.shape, const2),
            _resident(ws.shape, const2),
        ],
        out_specs=pl.BlockSpec((tm, d), row),
        out_shape=jax.ShapeDtypeStruct((m, d), BF16),
        scratch_shapes=[pltpu.VMEM((tm, w), BF16)],
        compiler_params=_params("arbitrary"),
        name="branches",
    )(retg, uvs, uvs, gates, gates, ln_g.reshape(1, w), ln_b.reshape(1, w), sgw, sgb_tab, wr, ws)


def _outproj_kernel(m_ref, wo_ref, x_ref, gt1_ref, g2_ref, sh2_ref, sc2_ref, x1_ref, h2_ref, *, tm):
    for r0 in range(0, tm, SUB_ROWS):
        rows = slice(r0, r0 + SUB_ROWS)
        mr = m_ref[rows, :]
        for n0 in range(0, x1_ref.shape[1], MXU_N):
            cols = slice(n0, n0 + MXU_N)
            x1_ref[rows, cols] = x_ref[rows, cols] + gt1_ref[0, :, cols] * _dot(mr, wo_ref[:, cols])
        h2_ref[rows, :] = _modulated_norm(x1_ref[rows, :], g2_ref[...], sh2_ref[0], sc2_ref[0])


def _outproj(merged, wo, x2d, mod3, g2, seq_len, tm):
    m, d = x2d.shape
    per = seq_len // tm
    row = lambda i: (i, 0)
    modspec = lambda blk: pl.BlockSpec((1, 1, d), lambda i: (i // per, 0, blk))
    return pl.pallas_call(
        functools.partial(_outproj_kernel, tm=tm),
        grid=(m // tm,),
        in_specs=[
            pl.BlockSpec((tm, d), row),
            _resident(wo.shape, lambda i: (0, 0)),
            pl.BlockSpec((tm, d), row),
            modspec(2), pl.BlockSpec((1, d), lambda i: (0, 0)), modspec(3), modspec(4),
        ],
        out_specs=[pl.BlockSpec((tm, d), row), pl.BlockSpec((tm, d), row)],
        out_shape=[jax.ShapeDtypeStruct((m, d), F32), jax.ShapeDtypeStruct((m, d), BF16)],
        compiler_params=_params("arbitrary"),
        name="outproj",
    )(merged, wo, x2d, mod3, g2.reshape(1, d), mod3, mod3)


def _ffn_up_kernel(h_ref, wa_ref, wb_ref, o_ref, *, tm):
    for r0 in range(0, tm, SUB_ROWS):
        rows = slice(r0, r0 + SUB_ROWS)
        hr = h_ref[rows, :]
        a = _dot(hr, wa_ref[...])
        b = _dot(hr, wb_ref[...])
        o_ref[rows, :] = (_silu(a) * b).astype(BF16)


def _ffn_up(h2, w_in, tm, tn):
    m, d = h2.shape
    hidden = w_in.shape[1] // 2
    nt = hidden // tn
    return pl.pallas_call(
        functools.partial(_ffn_up_kernel, tm=tm),
        grid=(m // tm, nt),
        in_specs=[
            pl.BlockSpec((tm, d), lambda i, j: (i, 0)),
            pl.BlockSpec((d, tn), lambda i, j: (0, j)),
            pl.BlockSpec((d, tn), lambda i, j: (0, nt + j)),
        ],
        out_specs=pl.BlockSpec((tm, tn), lambda i, j: (i, j)),
        out_shape=jax.ShapeDtypeStruct((m, hidden), BF16),
        compiler_params=_params("arbitrary", "arbitrary"),
        name="ffn_up",
    )(h2, w_in, w_in)


def _ffn_down_kernel(a_ref, w_ref, x1_ref, gt2_ref, gf_ref, o_ref, *, tm):
    for r0 in range(0, tm, SUB_ROWS):
        rows = slice(r0, r0 + SUB_ROWS)
        ar = a_ref[rows, :]
        for n0 in range(0, o_ref.shape[1], MXU_N):
            cols = slice(n0, n0 + MXU_N)
            o_ref[rows, cols] = x1_ref[rows, cols] + gt2_ref[0, :, cols] * _dot(ar, w_ref[:, cols])
        o_ref[rows, :] = _rms(o_ref[rows, :]) * gf_ref[...]


def _ffn_down(act, w_out, x1, mod3, gf, seq_len, tm):
    m, d = x1.shape
    hidden = act.shape[1]
    per = seq_len // tm
    row = lambda i: (i, 0)
    return pl.pallas_call(
        functools.partial(_ffn_down_kernel, tm=tm),
        grid=(m // tm,),
        in_specs=[
            pl.BlockSpec((tm, hidden), row),
            _resident(w_out.shape, lambda i: (0, 0)),
            pl.BlockSpec((tm, d), row),
            pl.BlockSpec((1, 1, d), lambda i: (i // per, 0, 5)),
            pl.BlockSpec((1, d), lambda i: (0, 0)),
        ],
        out_specs=pl.BlockSpec((tm, d), row),
        out_shape=jax.ShapeDtypeStruct((m, d), F32),
        compiler_params=_params("arbitrary"),
        name="ffn_down",
    )(act, w_out, x1, mod3, gf.reshape(1, d))


def _rope_tables(seq_len, dk):
    freqs = dk // 4
    rows = seq_len // GRID_W
    row = jnp.repeat(jnp.arange(rows), GRID_W)
    col = jnp.tile(jnp.arange(GRID_W), rows)
    freq = ROPE_BASE ** (-jnp.arange(freqs, dtype=F32) / freqs)
    ang = jnp.stack([row, col], axis=-1).astype(F32)[:, :, None] * freq
    cos, sin = jnp.cos(ang), jnp.sin(ang)
    cos_t = jnp.concatenate([cos[:, 0], cos[:, 0], cos[:, 1], cos[:, 1]], axis=-1)
    sin_t = jnp.concatenate([-sin[:, 0], sin[:, 0], -sin[:, 1], sin[:, 1]], axis=-1)
    return cos_t, sin_t


def kernel(x, c, ctx, c_ctx, w_mod, b_mod, norm1_g, w_in, ret_decay_fwd, ret_decay_bwd,
           sg_ln_g, sg_ln_b, sg_w, sg_b, w_ret_o, w_sg_o, w_out, norm2_g, w_ffn_in, w_ffn_out,
           final_norm_g):
    batch, seq_len, d = x.shape
    ctx_len = ctx.shape[1]
    depth = w_mod.shape[0]
    assert depth == 1
    width = w_ret_o.shape[1]
    dk = width // RET_HEADS
    q_off, k_off, v_off, g_off, u_off, gr_off = (i * width for i in (0, 1, 2, 3, 4, 6))

    x2d = x.reshape(batch * seq_len, d)
    ctx2d = ctx.reshape(batch * ctx_len, d)

    pad = (-(batch + 1)) % 8
    cond = jnp.concatenate([c, c_ctx[None], jnp.zeros((pad, d), F32)], axis=0)
    mod = _adaln(cond, w_mod[0], b_mod[0])
    mod3 = mod.reshape(mod.shape[0], 1, mod.shape[1])

    lg = jnp.stack([-jax.nn.softplus(-ret_decay_fwd[0].astype(F32)),
                    -jax.nn.softplus(-ret_decay_bwd[0].astype(F32))])

    hc = _prenorm(ctx2d, norm1_g[0], mod3, ctx_len, lambda b: batch, 0, 1, tl=ctx_len)
    w_in_b, kvc = _castproj(hc, w_in[0], k_off, 2 * width, width, dk ** -0.5)

    rope = _rope_tables(seq_len, dk)
    qk, h = _proj(x2d, w_in_b, q_off, 2 * width, "rope", tm=1024, tn=1024, rope=rope,
                  seq_len=seq_len, col_scale=(width // 1024, 2 * width // 1024, dk ** -0.5),
                  norm=(norm1_g[0], mod3, 0, 1), name="proj_qk")
    tm, tn = 2048, 1024
    v, wr_b, ws_b, wo_b = _proj(h, w_in_b, v_off, width, "none", tm=tm, tn=tn,
                                sides=(w_ret_o[0], w_sg_o[0], w_out[0]), name="proj_v")
    g, w_down_b = _proj(h, w_in_b, g_off, width, "silu", tm=tm, tn=tn, sides=(w_ffn_out[0],),
                        name="proj_g")
    uvs = _proj(h, w_in_b, u_off, 2 * width, "gelu", tm=tm, tn=tn, name="proj_uvs")
    gates, w_up_b = _proj(h, w_in_b, gr_off, 2 * d, "sigmoid", tm=tm, tn=tn, sides=(w_ffn_in[0],),
                          name="proj_gates")

    retg = _retention(lg, qk, v, g, kvc, batch, seq_len, ctx_len)

    sgb_tab = jnp.repeat(sg_b[0].T, width // SG_GROUPS, axis=1)
    merged = _branches(retg, uvs, gates, sg_ln_g[0], sg_ln_b[0], sg_w[0], sgb_tab, wr_b, ws_b,
                       tm=512)
    x1, h2 = _outproj(merged, wo_b, x2d, mod3, norm2_g[0], seq_len, tm=512)

    act = _ffn_up(h2, w_up_b, tm=2048, tn=512)
    out = _ffn_down(act, w_down_b, x1, mod3, final_norm_g, seq_len, tm=512)
    return out.reshape(batch, seq_len, d)
```

```python
import functools
import math

import jax
import jax.numpy as jnp
from jax import lax
from jax.experimental import pallas as pl
from jax.experimental.pallas import tpu as pltpu

F32 = jnp.float32
BF16 = jnp.bfloat16

EPS = 1e-6
GRID_W = 64
ROPE_BASE = 10000.0
RET_HEADS = 8
SG_GROUPS = 8
SG_CHUNK = 128
V7X_VMEM_BYTES = 64 * 1024 * 1024
VMEM_LIMIT = V7X_VMEM_BYTES - 8 * 1024 * 1024
MXU_N = 256
RET_C = MXU_N
SUB_ROWS = 256
PROJ_COLS = 1024


def _params(*sem):
    return pltpu.CompilerParams(dimension_semantics=sem, vmem_limit_bytes=VMEM_LIMIT)


def _sigmoid(x):
    return 1.0 / (1.0 + jnp.exp(-x))


def _silu(x):
    return x * _sigmoid(x)


def _gelu_tanh(x):
    b = -2.0 * math.sqrt(2.0 / math.pi) * math.log2(math.e)
    a = b * 0.044715
    return x / (1.0 + jnp.exp2(x * (a * (x * x) + b)))


def _rms(x):
    return x * lax.rsqrt(jnp.mean(x * x, axis=-1, keepdims=True) + EPS)


def _dot(a, b):
    return jnp.dot(a, b, preferred_element_type=F32)


def _resident(shape, index_map):
    return pl.BlockSpec(shape, index_map, pipeline_mode=pl.Buffered(1))


def _adaln_kernel(c_ref, w_ref, b_ref, o_ref):
    s = _silu(c_ref[...]).astype(BF16)
    o_ref[...] = _dot(s, w_ref[...].astype(BF16)) + b_ref[...]


def _adaln(cond, w_mod, b_mod, tn=1024):
    rows, d = cond.shape
    n = w_mod.shape[1]
    return pl.pallas_call(
        _adaln_kernel,
        grid=(n // tn,),
        in_specs=[
            pl.BlockSpec((rows, d), lambda j: (0, 0)),
            pl.BlockSpec((d, tn), lambda j: (0, j)),
            pl.BlockSpec((1, tn), lambda j: (0, j)),
        ],
        out_specs=pl.BlockSpec((rows, tn), lambda j: (0, j)),
        out_shape=jax.ShapeDtypeStruct((rows, n), F32),
        compiler_params=_params("arbitrary"),
        name="adaln",
    )(cond, w_mod, b_mod.reshape(1, n))


def _modulated_norm(x, g, sh, sc):
    return ((_rms(x) * g) * (1.0 + sc) + sh).astype(BF16)


def _prenorm_kernel(x_ref, g_ref, sh_ref, sc_ref, o_ref):
    o_ref[...] = _modulated_norm(x_ref[...], g_ref[...], sh_ref[0], sc_ref[0])


def _prenorm(x2d, g, mod3, rows_per_batch, mod_row, shift_blk, scale_blk, tl):
    m, d = x2d.shape
    per = rows_per_batch // tl
    return pl.pallas_call(
        _prenorm_kernel,
        grid=(m // tl,),
        in_specs=[
            pl.BlockSpec((tl, d), lambda i: (i, 0)),
            pl.BlockSpec((1, d), lambda i: (0, 0)),
            pl.BlockSpec((1, 1, d), lambda i: (mod_row(i // per), 0, shift_blk)),
            pl.BlockSpec((1, 1, d), lambda i: (mod_row(i // per), 0, scale_blk)),
        ],
        out_specs=pl.BlockSpec((tl, d), lambda i: (i, 0)),
        out_shape=jax.ShapeDtypeStruct((m, d), BF16),
        compiler_params=_params("arbitrary"),
        name="prenorm",
    )(x2d, g.reshape(1, d), mod3, mod3)


def _activate(acc, act):
    if act == "silu":
        return _silu(acc)
    if act == "gelu":
        return _gelu_tanh(acc)
    if act == "sigmoid":
        return _sigmoid(acc)
    return acc


def _proj_kernel(*refs, act, col_scale, tm, ncols, fused_norm, n_side):
    refs = list(refs)
    side_out = [refs.pop() for _ in range(n_side)][::-1]
    if fused_norm:
        x_ref, g_ref, sh_ref, sc_ref = refs[:4]
        refs = refs[4:]
        h_ref = refs.pop()
    else:
        h_ref = refs.pop(0)
    w_ref = refs.pop(0)
    o_ref = refs.pop()
    side_in = [refs.pop() for _ in range(n_side)][::-1]
    if act == "rope":
        cos_ref, sin_ref = refs
    for src, dst in zip(side_in, side_out):
        dst[...] = src[...].astype(BF16)
    half = MXU_N // 2
    for r0 in range(0, tm, SUB_ROWS):
        rows = slice(r0, r0 + SUB_ROWS)
        if fused_norm:
            h_ref[rows, :] = _modulated_norm(x_ref[rows, :], g_ref[...], sh_ref[0], sc_ref[0])
        for n0 in range(0, ncols, PROJ_COLS):
            acc = _dot(h_ref[rows, :], w_ref[:, n0:n0 + PROJ_COLS])
            if col_scale is not None and col_scale[0] <= n0 < col_scale[1]:
                acc = acc * col_scale[2]
            acc = _activate(acc, act)
            if act == "rope":
                for s in range(PROJ_COLS // half):
                    lanes = slice((s % 2) * half, (s % 2 + 1) * half)
                    xa = acc[:, s * half:(s + 1) * half]
                    ya = (xa * cos_ref[rows, lanes]
                          + pltpu.roll(xa, half // 2, axis=1) * sin_ref[rows, lanes])
                    o_ref[rows, n0 + s * half:n0 + (s + 1) * half] = ya.astype(BF16)
            else:
                o_ref[rows, n0:n0 + PROJ_COLS] = acc.astype(BF16)


def _proj(h, w, col0, ncols, act, *, tm, rope=None, seq_len=None, col_scale=None,
          norm=None, sides=(), name="proj"):
    m, k = h.shape
    assert col0 % ncols == 0
    steps = m // tm
    in_specs, args = [], []
    if norm is not None:
        g, mod3, shift_blk, scale_blk = norm
        per = seq_len // tm
        in_specs += [
            pl.BlockSpec((tm, k), lambda i: (i, 0)),
            pl.BlockSpec((1, k), lambda i: (0, 0)),
            pl.BlockSpec((1, 1, k), lambda i: (i // per, 0, shift_blk)),
            pl.BlockSpec((1, 1, k), lambda i: (i // per, 0, scale_blk)),
        ]
        args += [h, g.reshape(1, k), mod3, mod3]
    else:
        in_specs.append(pl.BlockSpec((tm, k), lambda i: (i, 0)))
        args.append(h)
    in_specs.append(_resident((k, ncols), lambda i: (0, col0 // ncols)))
    args.append(w)
    if act == "rope":
        per_l = seq_len // tm
        spec = pl.BlockSpec((tm, MXU_N), lambda i: (i % per_l, 0))
        in_specs += [spec, spec]
        args += list(rope)
    out_specs = [pl.BlockSpec((tm, ncols), lambda i: (i, 0))]
    out_shape = [jax.ShapeDtypeStruct((m, ncols), BF16)]
    if norm is not None:
        out_specs.append(pl.BlockSpec((tm, k), lambda i: (i, 0)))
        out_shape.append(jax.ShapeDtypeStruct((m, k), BF16))
    for side in sides:
        r, c = side.shape
        assert r % (steps * 16) == 0
        spec = pl.BlockSpec((r // steps, c), lambda i: (i, 0))
        in_specs.append(spec)
        args.append(side)
        out_specs.append(spec)
        out_shape.append(jax.ShapeDtypeStruct((r, c), BF16))
    outs = pl.pallas_call(
        functools.partial(_proj_kernel, act=act, col_scale=col_scale, tm=tm, ncols=ncols,
                          fused_norm=norm is not None, n_side=len(sides)),
        grid=(steps,),
        in_specs=in_specs,
        out_specs=out_specs,
        out_shape=out_shape,
        compiler_params=_params("arbitrary"),
        name=name,
    )(*args)
    return outs[0] if len(outs) == 1 else outs


def _castproj_kernel(hc_ref, w_ref, wb_ref, kv_ref, *, m, lo, hi, k_blocks, scale):
    j = pl.program_id(0)
    wb_ref[...] = w_ref[...].astype(BF16)

    @pl.when(jnp.logical_and(j >= lo, j < hi))
    def _():
        mult = jnp.where(j < lo + k_blocks, scale, 1.0).astype(F32)
        for r0 in range(0, m, SUB_ROWS):
            rows = slice(r0, r0 + SUB_ROWS)
            kv_ref[rows, :] = (_dot(hc_ref[rows, :], wb_ref[...]) * mult).astype(BF16)


def _castproj(hc, w_f32, k_col0, kv_cols, k_cols, scale, tn=1024):
    m, k = hc.shape
    n = w_f32.shape[1]
    lo, hi = k_col0 // tn, (k_col0 + kv_cols) // tn
    return pl.pallas_call(
        functools.partial(_castproj_kernel, m=m, lo=lo, hi=hi, k_blocks=k_cols // tn, scale=scale),
        grid=(n // tn,),
        in_specs=[
            _resident((m, k), lambda j: (0, 0)),
            pl.BlockSpec((k, tn), lambda j: (0, j)),
        ],
        out_specs=[
            pl.BlockSpec((k, tn), lambda j: (0, j)),
            pl.BlockSpec((m, tn), lambda j: (0, jnp.clip(j - lo, 0, hi - lo - 1))),
        ],
        out_shape=[jax.ShapeDtypeStruct((k, n), BF16), jax.ShapeDtypeStruct((m, kv_cols), BF16)],
        compiler_params=_params("arbitrary"),
        name="castproj_ctx",
    )(hc, w_f32)


def _dot_t(a, b):
    return lax.dot_general(a, b, (((0,), (0,)), ((), ())), preferred_element_type=F32)


def _dot_nt(a, b):
    return lax.dot_general(a, b, (((1,), (1,)), ((), ())), preferred_element_type=F32)


def _ret_kernel(lg_ref, q_ref, k_ref, v_ref, g_ref, kc_ref, vc_ref, o_ref, sf_ref, sb_ref,
                *, n_chunks, heads_per_step):
    c = RET_C
    ri = lax.broadcasted_iota(jnp.int32, (c, c), 0).astype(F32)
    ci = lax.broadcasted_iota(jnp.int32, (c, c), 1).astype(F32)
    diff = ri - ci
    zero = jnp.zeros((1, c), F32)

    def kv(kn, vn, dec):
        return _dot_t(kn * dec, vn)

    for t in range(heads_per_step):
        hd = pl.program_id(1) * heads_per_step + t
        hc = slice(t * c, (t + 1) * c)
        lgf = lg_ref[0, hd]
        lgb = lg_ref[1, hd]
        dmask = jnp.where(diff >= 0.0,
                          jnp.exp(lgf * jnp.maximum(diff, 0.0)),
                          jnp.exp(lgb * jnp.maximum(-diff, 0.0))).astype(BF16)
        qdf = jnp.exp(lgf * (ri + 1.0)).astype(BF16)
        qdb = jnp.exp(lgb * (c - ri)).astype(BF16)
        kdf = jnp.exp(lgf * (c - 1.0 - ri)).astype(BF16)
        kdb = jnp.exp(lgb * ri).astype(BF16)
        cdf = jnp.exp(zero + lgf * c)
        cdb = jnp.exp(zero + lgb * c)

        kc = kc_ref[:, hc]
        vc = vc_ref[:, hc]
        sf_ref[t] = kv(kc, vc, kdf)
        sb = kv(kc, vc, kdb)
        for n in reversed(range(n_chunks)):
            rows = slice(n * c, (n + 1) * c)
            sb_ref[t, n] = sb.astype(BF16)
            if n > 0:
                sb = sb * cdb + kv(k_ref[rows, hc], v_ref[rows, hc], kdb)
        for n in range(n_chunks):
            rows = slice(n * c, (n + 1) * c)
            qn = q_ref[rows, hc]
            kn = k_ref[rows, hc]
            vn = v_ref[rows, hc]
            a = _dot_nt(qn, kn).astype(BF16) * dmask
            o = (_dot(a, vn)
                 + _dot(qn * qdf, sf_ref[t].astype(BF16))
                 + _dot(qn * qdb, sb_ref[t, n]))
            o_ref[rows, hc] = _rms(o).astype(BF16) * g_ref[rows, hc]
            if n + 1 < n_chunks:
                sf_ref[t] = sf_ref[t] * cdf + kv(kn, vn, kdf)


def _retention(lg, qk, v, g, kvc, batch, seq_len, ctx_len, heads_per_step=2):
    m = qk.shape[0]
    hd = RET_HEADS
    dk = qk.shape[1] // (2 * hd)
    dv = v.shape[1] // hd
    assert dk == RET_C and dv == RET_C and ctx_len == RET_C and seq_len % RET_C == 0
    n_chunks = seq_len // RET_C
    hps = heads_per_step
    hb = hd // hps
    return pl.pallas_call(
        functools.partial(_ret_kernel, n_chunks=n_chunks, heads_per_step=hps),
        grid=(batch, hb),
        in_specs=[
            pl.BlockSpec(memory_space=pltpu.SMEM),
            pl.BlockSpec((seq_len, hps * dk), lambda b, h: (b, h)),
            pl.BlockSpec((seq_len, hps * dk), lambda b, h: (b, hb + h)),
            pl.BlockSpec((seq_len, hps * dv), lambda b, h: (b, h)),
            pl.BlockSpec((seq_len, hps * dv), lambda b, h: (b, h)),
            pl.BlockSpec((ctx_len, hps * dk), lambda b, h: (b, h)),
            pl.BlockSpec((ctx_len, hps * dv), lambda b, h: (b, hb + h)),
        ],
        out_specs=pl.BlockSpec((seq_len, hps * dv), lambda b, h: (b, h)),
        out_shape=jax.ShapeDtypeStruct((m, hd * dv), BF16),
        scratch_shapes=[
            pltpu.VMEM((hps, dk, dv), F32),
            pltpu.VMEM((hps, n_chunks, dk, dv), BF16),
        ],
        compiler_params=_params("arbitrary", "arbitrary"),
        name="retention",
    )(lg, qk, qk, v, g, kvc, kvc)


def _branch_kernel(ret_ref, u_ref, vs_ref, gr_ref, gs_ref, lng_ref, lnb_ref, sgw_ref, sgb_ref,
                   wr_ref, ws_ref, o_ref, sgo_ref, *, tm):
    width = vs_ref.shape[1]
    gd = width // SG_GROUPS
    for r0 in range(0, tm, SUB_ROWS):
        rows = slice(r0, r0 + SUB_ROWS)
        vs = vs_ref[rows, :].astype(F32)
        mu = jnp.mean(vs, axis=-1, keepdims=True)
        cen = vs - mu
        var = jnp.mean(cen * cen, axis=-1, keepdims=True)
        vn = ((cen * lax.rsqrt(var + EPS)) * lng_ref[...] + lnb_ref[...]).astype(BF16)
        for c0 in range(0, SUB_ROWS, SG_CHUNK):
            crow = slice(r0 + c0, r0 + c0 + SG_CHUNK)
            for gi in range(SG_GROUPS):
                cols = slice(gi * gd, (gi + 1) * gd)
                mixed = _dot(sgw_ref[gi].astype(BF16), vn[c0:c0 + SG_CHUNK, cols]) + sgb_ref[:, cols]
                sgo_ref[crow, cols] = (u_ref[crow, cols].astype(F32) * mixed).astype(BF16)
        ret = ret_ref[rows, :]
        sgo = sgo_ref[rows, :]
        for n0 in range(0, o_ref.shape[1], MXU_N):
            cols = slice(n0, n0 + MXU_N)
            yr = _dot(ret, wr_ref[:, cols])
            ys = _dot(sgo, ws_ref[:, cols])
            o_ref[rows, cols] = (gr_ref[rows, cols].astype(F32) * yr
                                 + gs_ref[rows, cols].astype(F32) * ys).astype(BF16)


def _branches(retg, uvs, gates, ln_g, ln_b, sgw, sgb_tab, wr, ws, tm):
    m, w = retg.shape
    d = wr.shape[1]
    row = lambda i: (i, 0)
    row1 = lambda i: (i, 1)
    const2 = lambda i: (0, 0)
    return pl.pallas_call(
        functools.partial(_branch_kernel, tm=tm),
        grid=(m // tm,),
        in_specs=[
            pl.BlockSpec((tm, w), row),
            pl.BlockSpec((tm, w), row),
            pl.BlockSpec((tm, w), row1),
            pl.BlockSpec((tm, d), row),
            pl.BlockSpec((tm, d), row1),
            _resident((1, w), const2),
            _resident((1, w), const2),
            _resident(sgw.shape, lambda i: (0, 0, 0)),
            _resident(sgb_tab.shape, const2),
            _resident(wr.shape, const2),
            _resident(ws.shape, const2),
        ],
        out_specs=pl.BlockSpec((tm, d), row),
        out_shape=jax.ShapeDtypeStruct((m, d), BF16),
        scratch_shapes=[pltpu.VMEM((tm, w), BF16)],
        compiler_params=_params("arbitrary"),
        name="branches",
    )(retg, uvs, uvs, gates, gates, ln_g.reshape(1, w), ln_b.reshape(1, w), sgw, sgb_tab, wr, ws)


def _outproj_kernel(m_ref, wo_ref, x_ref, gt1_ref, g2_ref, sh2_ref, sc2_ref, x1_ref, h2_ref, *, tm):
    for r0 in range(0, tm, SUB_ROWS):
        rows = slice(r0, r0 + SUB_ROWS)
        mr = m_ref[rows, :]
        for n0 in range(0, x1_ref.shape[1], MXU_N):
            cols = slice(n0, n0 + MXU_N)
            x1_ref[rows, cols] = x_ref[rows, cols] + gt1_ref[0, :, cols] * _dot(mr, wo_ref[:, cols])
        h2_ref[rows, :] = _modulated_norm(x1_ref[rows, :], g2_ref[...], sh2_ref[0], sc2_ref[0])


def _outproj(merged, wo, x2d, mod3, g2, seq_len, tm):
    m, d = x2d.shape
    per = seq_len // tm
    row = lambda i: (i, 0)
    modspec = lambda blk: pl.BlockSpec((1, 1, d), lambda i: (i // per, 0, blk))
    return pl.pallas_call(
        functools.partial(_outproj_kernel, tm=tm),
        grid=(m // tm,),
        in_specs=[
            pl.BlockSpec((tm, d), row),
            _resident(wo.shape, lambda i: (0, 0)),
            pl.BlockSpec((tm, d), row),
            modspec(2), pl.BlockSpec((1, d), lambda i: (0, 0)), modspec(3), modspec(4),
        ],
        out_specs=[pl.BlockSpec((tm, d), row), pl.BlockSpec((tm, d), row)],
        out_shape=[jax.ShapeDtypeStruct((m, d), F32), jax.ShapeDtypeStruct((m, d), BF16)],
        compiler_params=_params("arbitrary"),
        name="outproj",
    )(merged, wo, x2d, mod3, g2.reshape(1, d), mod3, mod3)


def _ffn_up_kernel(h_ref, wa_ref, wb_ref, o_ref, *, tm):
    for r0 in range(0, tm, SUB_ROWS):
        rows = slice(r0, r0 + SUB_ROWS)
        hr = h_ref[rows, :]
        a = _dot(hr, wa_ref[...])
        b = _dot(hr, wb_ref[...])
        o_ref[rows, :] = (_silu(a) * b).astype(BF16)


def _ffn_up(h2, w_in, tm, tn):
    m, d = h2.shape
    hidden = w_in.shape[1] // 2
    nt = hidden // tn
    return pl.pallas_call(
        functools.partial(_ffn_up_kernel, tm=tm),
        grid=(m // tm, nt),
        in_specs=[
            pl.BlockSpec((tm, d), lambda i, j: (i, 0)),
            pl.BlockSpec((d, tn), lambda i, j: (0, j)),
            pl.BlockSpec((d, tn), lambda i, j: (0, nt + j)),
        ],
        out_specs=pl.BlockSpec((tm, tn), lambda i, j: (i, j)),
        out_shape=jax.ShapeDtypeStruct((m, hidden), BF16),
        compiler_params=_params("arbitrary", "arbitrary"),
        name="ffn_up",
    )(h2, w_in, w_in)


def _ffn_down_kernel(a_ref, w_ref, x1_ref, gt2_ref, gf_ref, o_ref, *, tm):
    for r0 in range(0, tm, SUB_ROWS):
        rows = slice(r0, r0 + SUB_ROWS)
        ar = a_ref[rows, :]
        for n0 in range(0, o_ref.shape[1], MXU_N):
            cols = slice(n0, n0 + MXU_N)
            o_ref[rows, cols] = x1_ref[rows, cols] + gt2_ref[0, :, cols] * _dot(ar, w_ref[:, cols])
        o_ref[rows, :] = _rms(o_ref[rows, :]) * gf_ref[...]


def _ffn_down(act, w_out, x1, mod3, gf, seq_len, tm):
    m, d = x1.shape
    hidden = act.shape[1]
    per = seq_len // tm
    row = lambda i: (i, 0)
    return pl.pallas_call(
        functools.partial(_ffn_down_kernel, tm=tm),
        grid=(m // tm,),
        in_specs=[
            pl.BlockSpec((tm, hidden), row),
            _resident(w_out.shape, lambda i: (0, 0)),
            pl.BlockSpec((tm, d), row),
            pl.BlockSpec((1, 1, d), lambda i: (i // per, 0, 5)),
            pl.BlockSpec((1, d), lambda i: (0, 0)),
        ],
        out_specs=pl.BlockSpec((tm, d), row),
        out_shape=jax.ShapeDtypeStruct((m, d), F32),
        compiler_params=_params("arbitrary"),
        name="ffn_down",
    )(act, w_out, x1, mod3, gf.reshape(1, d))


def _rope_tables(seq_len, dk):
    freqs = dk // 4
    rows = seq_len // GRID_W
    row = jnp.repeat(jnp.arange(rows), GRID_W)
    col = jnp.tile(jnp.arange(GRID_W), rows)
    freq = ROPE_BASE ** (-jnp.arange(freqs, dtype=F32) / freqs)
    ang = jnp.stack([row, col], axis=-1).astype(F32)[:, :, None] * freq
    cos, sin = jnp.cos(ang), jnp.sin(ang)
    cos_t = jnp.concatenate([cos[:, 0], cos[:, 0], cos[:, 1], cos[:, 1]], axis=-1)
    sin_t = jnp.concatenate([-sin[:, 0], sin[:, 0], -sin[:, 1], sin[:, 1]], axis=-1)
    return cos_t, sin_t


def kernel(x, c, ctx, c_ctx, w_mod, b_mod, norm1_g, w_in, ret_decay_fwd, ret_decay_bwd,
           sg_ln_g, sg_ln_b, sg_w, sg_b, w_ret_o, w_sg_o, w_out, norm2_g, w_ffn_in, w_ffn_out,
           final_norm_g):
    batch, seq_len, d = x.shape
    ctx_len = ctx.shape[1]
    depth = w_mod.shape[0]
    assert depth == 1
    width = w_ret_o.shape[1]
    dk = width // RET_HEADS
    q_off, k_off, v_off, g_off, u_off, gr_off = (i * width for i in (0, 1, 2, 3, 4, 6))

    x2d = x.reshape(batch * seq_len, d)
    ctx2d = ctx.reshape(batch * ctx_len, d)

    pad = (-(batch + 1)) % 8
    cond = jnp.concatenate([c, c_ctx[None], jnp.zeros((pad, d), F32)], axis=0)
    mod = _adaln(cond, w_mod[0], b_mod[0])
    mod3 = mod.reshape(mod.shape[0], 1, mod.shape[1])

    lg = jnp.stack([-jax.nn.softplus(-ret_decay_fwd[0].astype(F32)),
                    -jax.nn.softplus(-ret_decay_bwd[0].astype(F32))])

    hc = _prenorm(ctx2d, norm1_g[0], mod3, ctx_len, lambda b: batch, 0, 1, tl=ctx_len)
    w_in_b, kvc = _castproj(hc, w_in[0], k_off, 2 * width, width, dk ** -0.5)

    rope = _rope_tables(seq_len, dk)
    qk, h = _proj(x2d, w_in_b, q_off, 2 * width, "rope", tm=512, rope=rope, seq_len=seq_len,
                  col_scale=(width, 2 * width, dk ** -0.5), norm=(norm1_g[0], mod3, 0, 1),
                  name="proj_qk")
    v, wr_b, ws_b, wo_b = _proj(h, w_in_b, v_off, width, "none", tm=1024,
                                sides=(w_ret_o[0], w_sg_o[0], w_out[0]), name="proj_v")
    g, w_down_b = _proj(h, w_in_b, g_off, width, "silu", tm=1024, sides=(w_ffn_out[0],),
                        name="proj_g")
    uvs = _proj(h, w_in_b, u_off, 2 * width, "gelu", tm=512, name="proj_uvs")
    gates, w_up_b = _proj(h, w_in_b, gr_off, 2 * d, "sigmoid", tm=512, sides=(w_ffn_in[0],),
                          name="proj_gates")

    retg = _retention(lg, qk, v, g, kvc, batch, seq_len, ctx_len)

    sgb_tab = jnp.repeat(sg_b[0].T, width // SG_GROUPS, axis=1)
    merged = _branches(retg, uvs, gates, sg_ln_g[0], sg_ln_b[0], sg_w[0], sgb_tab, wr_b, ws_b,
                       tm=512)
    x1, h2 = _outproj(merged, wo_b, x2d, mod3, norm2_g[0], seq_len, tm=512)

    act = _ffn_up(h2, w_up_b, tm=2048, tn=512)
    out = _ffn_down(act, w_down_b, x1, mod3, final_norm_g, seq_len, tm=512)
    return out.reshape(batch, seq_len, d)
```

```python
import functools
import math

import jax
import jax.numpy as jnp
import numpy as np
from jax import lax
from jax.experimental import pallas as pl
from jax.experimental.pallas import tpu as pltpu

F32 = jnp.float32
BF16 = jnp.bfloat16

EPS = 1e-6
GRID_W = 64
ROPE_BASE = 10000.0
RET_HEADS = 8
SG_GROUPS = 8
SG_CHUNK = 128
V7X_VMEM_BYTES = 64 * 1024 * 1024
VMEM_LIMIT = V7X_VMEM_BYTES - 8 * 1024 * 1024
MXU_N = 256
RET_C = MXU_N
SUB_ROWS = 256
PROJ_COLS = 1024


def _params(*sem):
    return pltpu.CompilerParams(dimension_semantics=sem, vmem_limit_bytes=VMEM_LIMIT)


def _sigmoid(x):
    return 1.0 / (1.0 + jnp.exp(-x))


def _silu(x):
    return x * _sigmoid(x)


def _gelu_tanh(x):
    b = -2.0 * math.sqrt(2.0 / math.pi) * math.log2(math.e)
    a = b * 0.044715
    return x / (1.0 + jnp.exp2(x * (a * (x * x) + b)))


def _rms(x):
    return x * lax.rsqrt(jnp.mean(x * x, axis=-1, keepdims=True) + EPS)


def _dot(a, b):
    return jnp.dot(a, b, preferred_element_type=F32)


def _resident(shape, index_map):
    return pl.BlockSpec(shape, index_map, pipeline_mode=pl.Buffered(1))


def _adaln_kernel(c_ref, w_ref, b_ref, o_ref):
    s = _silu(c_ref[...]).astype(BF16)
    o_ref[...] = _dot(s, w_ref[...].astype(BF16)) + b_ref[...]


def _adaln(cond, w_mod, b_mod, tn=1024):
    rows, d = cond.shape
    n = w_mod.shape[1]
    return pl.pallas_call(
        _adaln_kernel,
        grid=(n // tn,),
        in_specs=[
            pl.BlockSpec((rows, d), lambda j: (0, 0)),
            pl.BlockSpec((d, tn), lambda j: (0, j)),
            pl.BlockSpec((1, tn), lambda j: (0, j)),
        ],
        out_specs=pl.BlockSpec((rows, tn), lambda j: (0, j)),
        out_shape=jax.ShapeDtypeStruct((rows, n), F32),
        compiler_params=_params("arbitrary"),
        name="adaln",
    )(cond, w_mod, b_mod.reshape(1, n))


def _modulated_norm(x, g, sh, sc):
    return ((_rms(x) * g) * (1.0 + sc) + sh).astype(BF16)


def _prenorm_kernel(x_ref, g_ref, sh_ref, sc_ref, o_ref):
    o_ref[...] = _modulated_norm(x_ref[...], g_ref[...], sh_ref[0], sc_ref[0])


def _prenorm(x2d, g, mod3, rows_per_batch, mod_row, shift_blk, scale_blk, tl):
    m, d = x2d.shape
    per = rows_per_batch // tl
    return pl.pallas_call(
        _prenorm_kernel,
        grid=(m // tl,),
        in_specs=[
            pl.BlockSpec((tl, d), lambda i: (i, 0)),
            pl.BlockSpec((1, d), lambda i: (0, 0)),
            pl.BlockSpec((1, 1, d), lambda i: (mod_row(i // per), 0, shift_blk)),
            pl.BlockSpec((1, 1, d), lambda i: (mod_row(i // per), 0, scale_blk)),
        ],
        out_specs=pl.BlockSpec((tl, d), lambda i: (i, 0)),
        out_shape=jax.ShapeDtypeStruct((m, d), BF16),
        compiler_params=_params("arbitrary"),
        name="prenorm",
    )(x2d, g.reshape(1, d), mod3, mod3)


def _activate(acc, act):
    if act == "silu":
        return _silu(acc)
    if act == "gelu":
        return _gelu_tanh(acc)
    if act == "sigmoid":
        return _sigmoid(acc)
    return acc


def _proj_kernel(*refs, act, col_scale, tm, ncols, fused_norm, n_side):
    refs = list(refs)
    side_out = [refs.pop() for _ in range(n_side)][::-1]
    if fused_norm:
        x_ref, g_ref, sh_ref, sc_ref = refs[:4]
        refs = refs[4:]
        h_ref = refs.pop()
    else:
        h_ref = refs.pop(0)
    w_ref = refs.pop(0)
    o_ref = refs.pop()
    side_in = [refs.pop() for _ in range(n_side)][::-1]
    if act == "rope":
        cos_ref, sin_ref = refs
    for src, dst in zip(side_in, side_out):
        dst[...] = src[...].astype(BF16)
    half = MXU_N // 2
    for r0 in range(0, tm, SUB_ROWS):
        rows = slice(r0, r0 + SUB_ROWS)
        if fused_norm:
            h_ref[rows, :] = _modulated_norm(x_ref[rows, :], g_ref[...], sh_ref[0], sc_ref[0])
        for n0 in range(0, ncols, PROJ_COLS):
            acc = _dot(h_ref[rows, :], w_ref[:, n0:n0 + PROJ_COLS])
            if col_scale is not None and col_scale[0] <= n0 < col_scale[1]:
                acc = acc * col_scale[2]
            acc = _activate(acc, act)
            if act == "rope":
                for s in range(PROJ_COLS // half):
                    lanes = slice((s % 2) * half, (s % 2 + 1) * half)
                    xa = acc[:, s * half:(s + 1) * half]
                    ya = (xa * cos_ref[rows, lanes]
                          + pltpu.roll(xa, half // 2, axis=1) * sin_ref[rows, lanes])
                    o_ref[rows, n0 + s * half:n0 + (s + 1) * half] = ya.astype(BF16)
            else:
                o_ref[rows, n0:n0 + PROJ_COLS] = acc.astype(BF16)


def _proj(h, w, col0, ncols, act, *, tm, rope=None, seq_len=None, col_scale=None,
          norm=None, sides=(), name="proj"):
    m, k = h.shape
    assert col0 % ncols == 0
    steps = m // tm
    in_specs, args = [], []
    if norm is not None:
        g, mod3, shift_blk, scale_blk = norm
        per = seq_len // tm
        in_specs += [
            pl.BlockSpec((tm, k), lambda i: (i, 0)),
            pl.BlockSpec((1, k), lambda i: (0, 0)),
            pl.BlockSpec((1, 1, k), lambda i: (i // per, 0, shift_blk)),
            pl.BlockSpec((1, 1, k), lambda i: (i // per, 0, scale_blk)),
        ]
        args += [h, g.reshape(1, k), mod3, mod3]
    else:
        in_specs.append(pl.BlockSpec((tm, k), lambda i: (i, 0)))
        args.append(h)
    in_specs.append(_resident((k, ncols), lambda i: (0, col0 // ncols)))
    args.append(w)
    if act == "rope":
        per_l = seq_len // tm
        spec = pl.BlockSpec((tm, MXU_N), lambda i: (i % per_l, 0))
        in_specs += [spec, spec]
        args += list(rope)
    out_specs = [pl.BlockSpec((tm, ncols), lambda i: (i, 0))]
    out_shape = [jax.ShapeDtypeStruct((m, ncols), BF16)]
    if norm is not None:
        out_specs.append(pl.BlockSpec((tm, k), lambda i: (i, 0)))
        out_shape.append(jax.ShapeDtypeStruct((m, k), BF16))
    for side in sides:
        arr, c0, c = side if isinstance(side, tuple) else (side, 0, side.shape[1])
        r = arr.shape[0]
        assert r % (steps * 16) == 0 and c0 % c == 0
        in_specs.append(pl.BlockSpec((r // steps, c), lambda i, cb=c0 // c: (i, cb)))
        args.append(arr)
        out_specs.append(pl.BlockSpec((r // steps, c), lambda i: (i, 0)))
        out_shape.append(jax.ShapeDtypeStruct((r, c), BF16))
    outs = pl.pallas_call(
        functools.partial(_proj_kernel, act=act, col_scale=col_scale, tm=tm, ncols=ncols,
                          fused_norm=norm is not None, n_side=len(sides)),
        grid=(steps,),
        in_specs=in_specs,
        out_specs=out_specs,
        out_shape=out_shape,
        compiler_params=_params("arbitrary"),
        name=name,
    )(*args)
    return outs[0] if len(outs) == 1 else outs


def _castproj_kernel(hc_ref, w_ref, wb_ref, kv_ref, *, m, lo, hi, k_blocks, scale):
    j = pl.program_id(0)
    wb_ref[...] = w_ref[...].astype(BF16)

    @pl.when(jnp.logical_and(j >= lo, j < hi))
    def _():
        mult = jnp.where(j < lo + k_blocks, scale, 1.0).astype(F32)
        for r0 in range(0, m, SUB_ROWS):
            rows = slice(r0, r0 + SUB_ROWS)
            kv_ref[rows, :] = (_dot(hc_ref[rows, :], wb_ref[...]) * mult).astype(BF16)


def _castproj(hc, w_f32, n, k_col0, kv_cols, k_cols, scale, tn=1024):
    m, k = hc.shape
    lo, hi = k_col0 // tn, (k_col0 + kv_cols) // tn
    assert hi * tn <= n
    return pl.pallas_call(
        functools.partial(_castproj_kernel, m=m, lo=lo, hi=hi, k_blocks=k_cols // tn, scale=scale),
        grid=(n // tn,),
        in_specs=[
            _resident((m, k), lambda j: (0, 0)),
            pl.BlockSpec((k, tn), lambda j: (0, j)),
        ],
        out_specs=[
            pl.BlockSpec((k, tn), lambda j: (0, j)),
            pl.BlockSpec((m, tn), lambda j: (0, jnp.clip(j - lo, 0, hi - lo - 1))),
        ],
        out_shape=[jax.ShapeDtypeStruct((k, n), BF16), jax.ShapeDtypeStruct((m, kv_cols), BF16)],
        compiler_params=_params("arbitrary"),
        name="castproj_ctx",
    )(hc, w_f32)


def _dot_t(a, b):
    return lax.dot_general(a, b, (((0,), (0,)), ((), ())), preferred_element_type=F32)


def _dot_nt(a, b):
    return lax.dot_general(a, b, (((1,), (1,)), ((), ())), preferred_element_type=F32)


def _ret_kernel(lg_ref, q_ref, k_ref, v_ref, g_ref, kc_ref, vc_ref, o_ref, sf_ref, sb_ref,
                *, n_chunks, heads_per_step):
    c = RET_C
    ri = lax.broadcasted_iota(jnp.int32, (c, c), 0).astype(F32)
    ci = lax.broadcasted_iota(jnp.int32, (c, c), 1).astype(F32)
    diff = ri - ci
    zero = jnp.zeros((1, c), F32)

    def kv(kn, vn, dec):
        return _dot_t(kn * dec, vn)

    for t in range(heads_per_step):
        hd = pl.program_id(1) * heads_per_step + t
        hc = slice(t * c, (t + 1) * c)
        lgf = lg_ref[0, hd]
        lgb = lg_ref[1, hd]
        dmask = jnp.where(diff >= 0.0,
                          jnp.exp(lgf * jnp.maximum(diff, 0.0)),
                          jnp.exp(lgb * jnp.maximum(-diff, 0.0))).astype(BF16)
        qdf = jnp.exp(lgf * (ri + 1.0)).astype(BF16)
        qdb = jnp.exp(lgb * (c - ri)).astype(BF16)
        kdf = jnp.exp(lgf * (c - 1.0 - ri)).astype(BF16)
        kdb = jnp.exp(lgb * ri).astype(BF16)
        cdf = jnp.exp(zero + lgf * c)
        cdb = jnp.exp(zero + lgb * c)

        kc = kc_ref[:, hc]
        vc = vc_ref[:, hc]
        sf_ref[t] = kv(kc, vc, kdf)
        sb = kv(kc, vc, kdb)
        for n in reversed(range(n_chunks)):
            rows = slice(n * c, (n + 1) * c)
            sb_ref[t, n] = sb.astype(BF16)
            if n > 0:
                sb = sb * cdb + kv(k_ref[rows, hc], v_ref[rows, hc], kdb)
        for n in range(n_chunks):
            rows = slice(n * c, (n + 1) * c)
            qn = q_ref[rows, hc]
            kn = k_ref[rows, hc]
            vn = v_ref[rows, hc]
            a = _dot_nt(qn, kn).astype(BF16) * dmask
            o = (_dot(a, vn)
                 + _dot(qn * qdf, sf_ref[t].astype(BF16))
                 + _dot(qn * qdb, sb_ref[t, n]))
            o_ref[rows, hc] = _rms(o).astype(BF16) * g_ref[rows, hc]
            if n + 1 < n_chunks:
                sf_ref[t] = sf_ref[t] * cdf + kv(kn, vn, kdf)


def _retention(lg, qk, v, g, kvc, batch, seq_len, ctx_len, heads_per_step=2):
    m = qk.shape[0]
    hd = RET_HEADS
    dk = qk.shape[1] // (2 * hd)
    dv = v.shape[1] // hd
    assert dk == RET_C and dv == RET_C and ctx_len == RET_C and seq_len % RET_C == 0
    n_chunks = seq_len // RET_C
    hps = heads_per_step
    hb = hd // hps
    return pl.pallas_call(
        functools.partial(_ret_kernel, n_chunks=n_chunks, heads_per_step=hps),
        grid=(batch, hb),
        in_specs=[
            pl.BlockSpec(memory_space=pltpu.SMEM),
            pl.BlockSpec((seq_len, hps * dk), lambda b, h: (b, h)),
            pl.BlockSpec((seq_len, hps * dk), lambda b, h: (b, hb + h)),
            pl.BlockSpec((seq_len, hps * dv), lambda b, h: (b, h)),
            pl.BlockSpec((seq_len, hps * dv), lambda b, h: (b, h)),
            pl.BlockSpec((ctx_len, hps * dk), lambda b, h: (b, h)),
            pl.BlockSpec((ctx_len, hps * dv), lambda b, h: (b, hb + h)),
        ],
        out_specs=pl.BlockSpec((seq_len, hps * dv), lambda b, h: (b, h)),
        out_shape=jax.ShapeDtypeStruct((m, hd * dv), BF16),
        scratch_shapes=[
            pltpu.VMEM((hps, dk, dv), F32),
            pltpu.VMEM((hps, n_chunks, dk, dv), BF16),
        ],
        compiler_params=_params("arbitrary", "arbitrary"),
        name="retention",
    )(lg, qk, qk, v, g, kvc, kvc)


def _branch_kernel(ret_ref, u_ref, vs_ref, gr_ref, gs_ref, lng_ref, lnb_ref, sgw_ref, sgb_ref,
                   wr_ref, ws_ref, o_ref, sgo_ref, *, tm):
    width = vs_ref.shape[1]
    gd = width // SG_GROUPS
    for r0 in range(0, tm, SUB_ROWS):
        rows = slice(r0, r0 + SUB_ROWS)
        vs = vs_ref[rows, :].astype(F32)
        mu = jnp.mean(vs, axis=-1, keepdims=True)
        cen = vs - mu
        var = jnp.mean(cen * cen, axis=-1, keepdims=True)
        vn = ((cen * lax.rsqrt(var + EPS)) * lng_ref[...] + lnb_ref[...]).astype(BF16)
        for c0 in range(0, SUB_ROWS, SG_CHUNK):
            crow = slice(r0 + c0, r0 + c0 + SG_CHUNK)
            for gi in range(SG_GROUPS):
                cols = slice(gi * gd, (gi + 1) * gd)
                mixed = _dot(sgw_ref[gi].astype(BF16), vn[c0:c0 + SG_CHUNK, cols]) + sgb_ref[:, cols]
                sgo_ref[crow, cols] = (u_ref[crow, cols].astype(F32) * mixed).astype(BF16)
        ret = ret_ref[rows, :]
        sgo = sgo_ref[rows, :]
        for n0 in range(0, o_ref.shape[1], MXU_N):
            cols = slice(n0, n0 + MXU_N)
            yr = _dot(ret, wr_ref[:, cols])
            ys = _dot(sgo, ws_ref[:, cols])
            o_ref[rows, cols] = (gr_ref[rows, cols].astype(F32) * yr
                                 + gs_ref[rows, cols].astype(F32) * ys).astype(BF16)


def _branches(retg, uvs, gates, ln_g, ln_b, sgw, sgb_tab, wr, ws, tm):
    m, w = retg.shape
    d = wr.shape[1]
    row = lambda i: (i, 0)
    row1 = lambda i: (i, 1)
    const2 = lambda i: (0, 0)
    return pl.pallas_call(
        functools.partial(_branch_kernel, tm=tm),
        grid=(m // tm,),
        in_specs=[
            pl.BlockSpec((tm, w), row),
            pl.BlockSpec((tm, w), row),
            pl.BlockSpec((tm, w), row1),
            pl.BlockSpec((tm, d), row),
            pl.BlockSpec((tm, d), row1),
            _resident((1, w), const2),
            _resident((1, w), const2),
            _resident(sgw.shape, lambda i: (0, 0, 0)),
            _resident(sgb_tab.shape, const2),
            _resident(wr.shape, const2),
            _resident(ws.shape, const2),
        ],
        out_specs=pl.BlockSpec((tm, d), row),
        out_shape=jax.ShapeDtypeStruct((m, d), BF16),
        scratch_shapes=[pltpu.VMEM((tm, w), BF16)],
        compiler_params=_params("arbitrary"),
        name="branches",
    )(retg, uvs, uvs, gates, gates, ln_g.reshape(1, w), ln_b.reshape(1, w), sgw, sgb_tab, wr, ws)


def _outproj_kernel(m_ref, wo_ref, x_ref, gt1_ref, g2_ref, sh2_ref, sc2_ref, x1_ref, h2_ref, *, tm):
    for r0 in range(0, tm, SUB_ROWS):
        rows = slice(r0, r0 + SUB_ROWS)
        mr = m_ref[rows, :]
        for n0 in range(0, x1_ref.shape[1], MXU_N):
            cols = slice(n0, n0 + MXU_N)
            x1_ref[rows, cols] = x_ref[rows, cols] + gt1_ref[0, :, cols] * _dot(mr, wo_ref[:, cols])
        h2_ref[rows, :] = _modulated_norm(x1_ref[rows, :], g2_ref[...], sh2_ref[0], sc2_ref[0])


def _outproj(merged, wo, x2d, mod3, g2, seq_len, tm):
    m, d = x2d.shape
    per = seq_len // tm
    row = lambda i: (i, 0)
    modspec = lambda blk: pl.BlockSpec((1, 1, d), lambda i: (i // per, 0, blk))
    return pl.pallas_call(
        functools.partial(_outproj_kernel, tm=tm),
        grid=(m // tm,),
        in_specs=[
            pl.BlockSpec((tm, d), row),
            _resident(wo.shape, lambda i: (0, 0)),
            pl.BlockSpec((tm, d), row),
            modspec(2), pl.BlockSpec((1, d), lambda i: (0, 0)), modspec(3), modspec(4),
        ],
        out_specs=[pl.BlockSpec((tm, d), row), pl.BlockSpec((tm, d), row)],
        out_shape=[jax.ShapeDtypeStruct((m, d), F32), jax.ShapeDtypeStruct((m, d), BF16)],
        compiler_params=_params("arbitrary"),
        name="outproj",
    )(merged, wo, x2d, mod3, g2.reshape(1, d), mod3, mod3)


def _ffn_up_kernel(h_ref, wa_ref, wb_ref, o_ref, *, tm):
    for r0 in range(0, tm, SUB_ROWS):
        rows = slice(r0, r0 + SUB_ROWS)
        hr = h_ref[rows, :]
        a = _dot(hr, wa_ref[...])
        b = _dot(hr, wb_ref[...])
        o_ref[rows, :] = (_silu(a) * b).astype(BF16)


def _ffn_up(h2, w_in, tm, tn):
    m, d = h2.shape
    hidden = w_in.shape[1] // 2
    nt = hidden // tn
    return pl.pallas_call(
        functools.partial(_ffn_up_kernel, tm=tm),
        grid=(m // tm, nt),
        in_specs=[
            pl.BlockSpec((tm, d), lambda i, j: (i, 0)),
            pl.BlockSpec((d, tn), lambda i, j: (0, j)),
            pl.BlockSpec((d, tn), lambda i, j: (0, nt + j)),
        ],
        out_specs=pl.BlockSpec((tm, tn), lambda i, j: (i, j)),
        out_shape=jax.ShapeDtypeStruct((m, hidden), BF16),
        compiler_params=_params("arbitrary", "arbitrary"),
        name="ffn_up",
    )(h2, w_in, w_in)


def _ffn_down_kernel(a_ref, w_ref, x1_ref, gt2_ref, gf_ref, o_ref, *, tm):
    for r0 in range(0, tm, SUB_ROWS):
        rows = slice(r0, r0 + SUB_ROWS)
        ar = a_ref[rows, :]
        for n0 in range(0, o_ref.shape[1], MXU_N):
            cols = slice(n0, n0 + MXU_N)
            o_ref[rows, cols] = x1_ref[rows, cols] + gt2_ref[0, :, cols] * _dot(ar, w_ref[:, cols])
        o_ref[rows, :] = _rms(o_ref[rows, :]) * gf_ref[...]


def _ffn_down(act, w_out, x1, mod3, gf, seq_len, tm):
    m, d = x1.shape
    hidden = act.shape[1]
    per = seq_len // tm
    row = lambda i: (i, 0)
    return pl.pallas_call(
        functools.partial(_ffn_down_kernel, tm=tm),
        grid=(m // tm,),
        in_specs=[
            pl.BlockSpec((tm, hidden), row),
            _resident(w_out.shape, lambda i: (0, 0)),
            pl.BlockSpec((tm, d), row),
            pl.BlockSpec((1, 1, d), lambda i: (i // per, 0, 5)),
            pl.BlockSpec((1, d), lambda i: (0, 0)),
        ],
        out_specs=pl.BlockSpec((tm, d), row),
        out_shape=jax.ShapeDtypeStruct((m, d), F32),
        compiler_params=_params("arbitrary"),
        name="ffn_down",
    )(act, w_out, x1, mod3, gf.reshape(1, d))


def _rope_tables(seq_len, dk):
    freqs = dk // 4
    rows = seq_len // GRID_W
    row = np.repeat(np.arange(rows), GRID_W)
    col = np.tile(np.arange(GRID_W), rows)
    freq = ROPE_BASE ** (-np.arange(freqs, dtype=np.float64) / freqs)
    ang = np.stack([row, col], axis=-1).astype(np.float64)[:, :, None] * freq
    cos, sin = np.cos(ang), np.sin(ang)
    cos_t = np.concatenate([cos[:, 0], cos[:, 0], cos[:, 1], cos[:, 1]], axis=-1)
    sin_t = np.concatenate([-sin[:, 0], sin[:, 0], -sin[:, 1], sin[:, 1]], axis=-1)
    return jnp.asarray(cos_t, F32), jnp.asarray(sin_t, F32)


def kernel(x, c, ctx, c_ctx, w_mod, b_mod, norm1_g, w_in, ret_decay_fwd, ret_decay_bwd,
           sg_ln_g, sg_ln_b, sg_w, sg_b, w_ret_o, w_sg_o, w_out, norm2_g, w_ffn_in, w_ffn_out,
           final_norm_g):
    batch, seq_len, d = x.shape
    ctx_len = ctx.shape[1]
    depth = w_mod.shape[0]
    assert depth == 1
    width = w_ret_o.shape[1]
    dk = width // RET_HEADS
    q_off, k_off, v_off, g_off, u_off, gr_off = (i * width for i in (0, 1, 2, 3, 4, 6))

    x2d = x.reshape(batch * seq_len, d)
    ctx2d = ctx.reshape(batch * ctx_len, d)

    pad = (-(batch + 1)) % 8
    cond = jnp.concatenate([c, c_ctx[None], jnp.zeros((pad, d), F32)], axis=0)
    mod = _adaln(cond, w_mod[0], b_mod[0])
    mod3 = mod.reshape(mod.shape[0], 1, mod.shape[1])

    lg = jnp.stack([-jax.nn.softplus(-ret_decay_fwd[0].astype(F32)),
                    -jax.nn.softplus(-ret_decay_bwd[0].astype(F32))])

    hc = _prenorm(ctx2d, norm1_g[0], mod3, ctx_len, lambda b: batch, 0, 1, tl=ctx_len)
    w_qkv_b, kvc = _castproj(hc, w_in[0], g_off, k_off, 2 * width, width, dk ** -0.5)

    rope = _rope_tables(seq_len, dk)
    qk, h, w_g_b, w_uvs_b, w_gates_b = _proj(
        x2d, w_qkv_b, q_off, 2 * width, "rope", tm=512, rope=rope, seq_len=seq_len,
        col_scale=(width, 2 * width, dk ** -0.5), norm=(norm1_g[0], mod3, 0, 1),
        sides=((w_in[0], g_off, width), (w_in[0], u_off, 2 * width), (w_in[0], gr_off, 2 * d)),
        name="proj_qk")
    v, wr_b, ws_b, wo_b = _proj(h, w_qkv_b, v_off, width, "none", tm=1024,
                                sides=(w_ret_o[0], w_sg_o[0], w_out[0]), name="proj_v")
    g, w_down_b = _proj(h, w_g_b, 0, width, "silu", tm=1024, sides=(w_ffn_out[0],),
                        name="proj_g")
    uvs = _proj(h, w_uvs_b, 0, 2 * width, "gelu", tm=512, name="proj_uvs")
    gates, w_up_b = _proj(h, w_gates_b, 0, 2 * d, "sigmoid", tm=512, sides=(w_ffn_in[0],),
                          name="proj_gates")

    retg = _retention(lg, qk, v, g, kvc, batch, seq_len, ctx_len)

    sgb_tab = jnp.repeat(sg_b[0].T, width // SG_GROUPS, axis=1)
    merged = _branches(retg, uvs, gates, sg_ln_g[0], sg_ln_b[0], sg_w[0], sgb_tab, wr_b, ws_b,
                       tm=512)
    x1, h2 = _outproj(merged, wo_b, x2d, mod3, norm2_g[0], seq_len, tm=512)

    act = _ffn_up(h2, w_up_b, tm=4096, tn=512)
    out = _ffn_down(act, w_down_b, x1, mod3, final_norm_g, seq_len, tm=512)
    return out.reshape(batch, seq_len, d)
```

```python
import functools
import math

import jax
import jax.numpy as jnp
import numpy as np
from jax import lax
from jax.experimental import pallas as pl
from jax.experimental.pallas import tpu as pltpu

F32 = jnp.float32
BF16 = jnp.bfloat16

EPS = 1e-6
GRID_W = 64
ROPE_BASE = 10000.0
RET_HEADS = 8
SG_GROUPS = 8
SG_CHUNK = 128
V7X_VMEM_BYTES = 64 * 1024 * 1024
VMEM_LIMIT = V7X_VMEM_BYTES - 8 * 1024 * 1024
MXU_N = 256
RET_C = MXU_N
SUB_ROWS = 256
WEIGHT_COLS = 512


def _params(*sem):
    return pltpu.CompilerParams(dimension_semantics=sem, vmem_limit_bytes=VMEM_LIMIT)


def _sigmoid(x):
    return 1.0 / (1.0 + jnp.exp(-x))


def _silu(x):
    return x * _sigmoid(x)


def _gelu_tanh(x):
    b = -2.0 * math.sqrt(2.0 / math.pi) * math.log2(math.e)
    a = b * 0.044715
    return x / (1.0 + jnp.exp2(x * (a * (x * x) + b)))


def _rms(x):
    return x * lax.rsqrt(jnp.mean(x * x, axis=-1, keepdims=True) + EPS)


def _dot(a, b):
    return jnp.dot(a, b, preferred_element_type=F32)


def _resident(shape, index_map):
    return pl.BlockSpec(shape, index_map, pipeline_mode=pl.Buffered(1))


def _weight_slabs(w, col0, ncols):
    assert col0 % WEIGHT_COLS == 0 and ncols % WEIGHT_COLS == 0
    n = ncols // WEIGHT_COLS
    specs = [_resident((w.shape[0], WEIGHT_COLS), lambda i, t=t: (0, col0 // WEIGHT_COLS + t))
             for t in range(n)]
    return specs, [w] * n


def _adaln_kernel(c_ref, w_ref, b_ref, o_ref):
    s = _silu(c_ref[...]).astype(BF16)
    o_ref[...] = _dot(s, w_ref[...].astype(BF16)) + b_ref[...]


def _adaln(cond, w_mod, b_mod, tn=1024):
    rows, d = cond.shape
    n = w_mod.shape[1]
    return pl.pallas_call(
        _adaln_kernel,
        grid=(n // tn,),
        in_specs=[
            pl.BlockSpec((rows, d), lambda j: (0, 0)),
            pl.BlockSpec((d, tn), lambda j: (0, j)),
            pl.BlockSpec((1, tn), lambda j: (0, j)),
        ],
        out_specs=pl.BlockSpec((rows, tn), lambda j: (0, j)),
        out_shape=jax.ShapeDtypeStruct((rows, n), F32),
        compiler_params=_params("arbitrary"),
        name="adaln",
    )(cond, w_mod, b_mod.reshape(1, n))


def _modulated_norm(x, g, sh, sc):
    return ((_rms(x) * g) * (1.0 + sc) + sh).astype(BF16)


def _prenorm_kernel(x_ref, g_ref, sh_ref, sc_ref, o_ref):
    o_ref[...] = _modulated_norm(x_ref[...], g_ref[...], sh_ref[0], sc_ref[0])


def _prenorm(x2d, g, mod3, rows_per_batch, mod_row, shift_blk, scale_blk, tl):
    m, d = x2d.shape
    per = rows_per_batch // tl
    return pl.pallas_call(
        _prenorm_kernel,
        grid=(m // tl,),
        in_specs=[
            pl.BlockSpec((tl, d), lambda i: (i, 0)),
            pl.BlockSpec((1, d), lambda i: (0, 0)),
            pl.BlockSpec((1, 1, d), lambda i: (mod_row(i // per), 0, shift_blk)),
            pl.BlockSpec((1, 1, d), lambda i: (mod_row(i // per), 0, scale_blk)),
        ],
        out_specs=pl.BlockSpec((tl, d), lambda i: (i, 0)),
        out_shape=jax.ShapeDtypeStruct((m, d), BF16),
        compiler_params=_params("arbitrary"),
        name="prenorm",
    )(x2d, g.reshape(1, d), mod3, mod3)


def _activate(acc, act):
    if act == "silu":
        return _silu(acc)
    if act == "gelu":
        return _gelu_tanh(acc)
    if act == "sigmoid":
        return _sigmoid(acc)
    return acc


def _proj_kernel(*refs, act, col_scale, tm, ncols, fused_norm, n_side):
    refs = list(refs)
    side_out = [refs.pop() for _ in range(n_side)][::-1]
    if fused_norm:
        x_ref, g_ref, sh_ref, sc_ref = refs[:4]
        refs = refs[4:]
        h_ref = refs.pop()
    else:
        h_ref = refs.pop(0)
    w_refs = [refs.pop(0) for _ in range(ncols // WEIGHT_COLS)]
    o_ref = refs.pop()
    side_in = [refs.pop() for _ in range(n_side)][::-1]
    if act == "rope":
        cos_ref, sin_ref = refs
    for src, dst in zip(side_in, side_out):
        dst[...] = src[...].astype(BF16)
    half = MXU_N // 2
    for r0 in range(0, tm, SUB_ROWS):
        rows = slice(r0, r0 + SUB_ROWS)
        if fused_norm:
            h_ref[rows, :] = _modulated_norm(x_ref[rows, :], g_ref[...], sh_ref[0], sc_ref[0])
        hr = h_ref[rows, :]
        for n0 in range(0, ncols, WEIGHT_COLS):
            acc = _dot(hr, w_refs[n0 // WEIGHT_COLS][...])
            if col_scale is not None and col_scale[0] <= n0 < col_scale[1]:
                acc = acc * col_scale[2]
            acc = _activate(acc, act)
            if act == "rope":
                for s in range(WEIGHT_COLS // half):
                    lanes = slice((s % 2) * half, (s % 2 + 1) * half)
                    xa = acc[:, s * half:(s + 1) * half]
                    ya = (xa * cos_ref[rows, lanes]
                          + pltpu.roll(xa, half // 2, axis=1) * sin_ref[rows, lanes])
                    o_ref[rows, n0 + s * half:n0 + (s + 1) * half] = ya.astype(BF16)
            else:
                o_ref[rows, n0:n0 + WEIGHT_COLS] = acc.astype(BF16)


def _proj(h, w, col0, ncols, act, *, tm, rope=None, seq_len=None, col_scale=None,
          norm=None, sides=(), name="proj"):
    m, k = h.shape
    steps = m // tm
    in_specs, args = [], []
    if norm is not None:
        g, mod3, shift_blk, scale_blk = norm
        per = seq_len // tm
        in_specs += [
            pl.BlockSpec((tm, k), lambda i: (i, 0)),
            pl.BlockSpec((1, k), lambda i: (0, 0)),
            pl.BlockSpec((1, 1, k), lambda i: (i // per, 0, shift_blk)),
            pl.BlockSpec((1, 1, k), lambda i: (i // per, 0, scale_blk)),
        ]
        args += [h, g.reshape(1, k), mod3, mod3]
    else:
        in_specs.append(pl.BlockSpec((tm, k), lambda i: (i, 0)))
        args.append(h)
    w_specs, w_args = _weight_slabs(w, col0, ncols)
    in_specs += w_specs
    args += w_args
    if act == "rope":
        per_l = seq_len // tm
        spec = pl.BlockSpec((tm, MXU_N), lambda i: (i % per_l, 0))
        in_specs += [spec, spec]
        args += list(rope)
    out_specs = [pl.BlockSpec((tm, ncols), lambda i: (i, 0))]
    out_shape = [jax.ShapeDtypeStruct((m, ncols), BF16)]
    if norm is not None:
        out_specs.append(pl.BlockSpec((tm, k), lambda i: (i, 0)))
        out_shape.append(jax.ShapeDtypeStruct((m, k), BF16))
    for side in sides:
        arr, c0, c = side if isinstance(side, tuple) else (side, 0, side.shape[1])
        r = arr.shape[0]
        assert r % (steps * 16) == 0 and c0 % c == 0
        in_specs.append(pl.BlockSpec((r // steps, c), lambda i, cb=c0 // c: (i, cb)))
        args.append(arr)
        out_specs.append(pl.BlockSpec((r // steps, c), lambda i: (i, 0)))
        out_shape.append(jax.ShapeDtypeStruct((r, c), BF16))
    outs = pl.pallas_call(
        functools.partial(_proj_kernel, act=act, col_scale=col_scale, tm=tm, ncols=ncols,
                          fused_norm=norm is not None, n_side=len(sides)),
        grid=(steps,),
        in_specs=in_specs,
        out_specs=out_specs,
        out_shape=out_shape,
        compiler_params=_params("arbitrary"),
        name=name,
    )(*args)
    return outs[0] if len(outs) == 1 else outs


def _castproj_kernel(hc_ref, w_ref, wb_ref, kv_ref, *, m, lo, hi, k_blocks, scale):
    j = pl.program_id(0)
    wb_ref[...] = w_ref[...].astype(BF16)

    @pl.when(jnp.logical_and(j >= lo, j < hi))
    def _():
        mult = jnp.where(j < lo + k_blocks, scale, 1.0).astype(F32)
        for r0 in range(0, m, SUB_ROWS):
            rows = slice(r0, r0 + SUB_ROWS)
            kv_ref[rows, :] = (_dot(hc_ref[rows, :], wb_ref[...]) * mult).astype(BF16)


def _castproj(hc, w_f32, n, k_col0, kv_cols, k_cols, scale, tn=1024):
    m, k = hc.shape
    lo, hi = k_col0 // tn, (k_col0 + kv_cols) // tn
    assert hi * tn <= n
    return pl.pallas_call(
        functools.partial(_castproj_kernel, m=m, lo=lo, hi=hi, k_blocks=k_cols // tn, scale=scale),
        grid=(n // tn,),
        in_specs=[
            _resident((m, k), lambda j: (0, 0)),
            pl.BlockSpec((k, tn), lambda j: (0, j)),
        ],
        out_specs=[
            pl.BlockSpec((k, tn), lambda j: (0, j)),
            pl.BlockSpec((m, tn), lambda j: (0, jnp.clip(j - lo, 0, hi - lo - 1))),
        ],
        out_shape=[jax.ShapeDtypeStruct((k, n), BF16), jax.ShapeDtypeStruct((m, kv_cols), BF16)],
        compiler_params=_params("arbitrary"),
        name="castproj_ctx",
    )(hc, w_f32)


def _dot_t(a, b):
    return lax.dot_general(a, b, (((0,), (0,)), ((), ())), preferred_element_type=F32)


def _dot_nt(a, b):
    return lax.dot_general(a, b, (((1,), (1,)), ((), ())), preferred_element_type=F32)


def _ret_kernel(lg_ref, q_ref, k_ref, v_ref, g_ref, kc_ref, vc_ref, o_ref, sf_ref, sb_ref,
                *, n_chunks, heads_per_step):
    c = RET_C
    ri = lax.broadcasted_iota(jnp.int32, (c, c), 0).astype(F32)
    ci = lax.broadcasted_iota(jnp.int32, (c, c), 1).astype(F32)
    diff = ri - ci
    zero = jnp.zeros((1, c), F32)

    def kv(kn, vn, dec):
        return _dot_t(kn * dec, vn)

    for t in range(heads_per_step):
        hd = pl.program_id(1) * heads_per_step + t
        hc = slice(t * c, (t + 1) * c)
        lgf = lg_ref[0, hd]
        lgb = lg_ref[1, hd]
        dmask = jnp.where(diff >= 0.0,
                          jnp.exp(lgf * jnp.maximum(diff, 0.0)),
                          jnp.exp(lgb * jnp.maximum(-diff, 0.0))).astype(BF16)
        qdf = jnp.exp(lgf * (ri + 1.0)).astype(BF16)
        qdb = jnp.exp(lgb * (c - ri)).astype(BF16)
        kdf = jnp.exp(lgf * (c - 1.0 - ri)).astype(BF16)
        kdb = jnp.exp(lgb * ri).astype(BF16)
        cdf = jnp.exp(zero + lgf * c)
        cdb = jnp.exp(zero + lgb * c)

        kc = kc_ref[:, hc]
        vc = vc_ref[:, hc]
        sf_ref[t] = kv(kc, vc, kdf)
        sb = kv(kc, vc, kdb)
        for n in reversed(range(n_chunks)):
            rows = slice(n * c, (n + 1) * c)
            sb_ref[t, n] = sb.astype(BF16)
            if n > 0:
                sb = sb * cdb + kv(k_ref[rows, hc], v_ref[rows, hc], kdb)
        for n in range(n_chunks):
            rows = slice(n * c, (n + 1) * c)
            qn = q_ref[rows, hc]
            kn = k_ref[rows, hc]
            vn = v_ref[rows, hc]
            a = _dot_nt(qn, kn).astype(BF16) * dmask
            o = (_dot(a, vn)
                 + _dot(qn * qdf, sf_ref[t].astype(BF16))
                 + _dot(qn * qdb, sb_ref[t, n]))
            o_ref[rows, hc] = _rms(o).astype(BF16) * g_ref[rows, hc]
            if n + 1 < n_chunks:
                sf_ref[t] = sf_ref[t] * cdf + kv(kn, vn, kdf)


def _retention(lg, qk, v, g, kvc, batch, seq_len, ctx_len, heads_per_step=2):
    m = qk.shape[0]
    hd = RET_HEADS
    dk = qk.shape[1] // (2 * hd)
    dv = v.shape[1] // hd
    assert dk == RET_C and dv == RET_C and ctx_len == RET_C and seq_len % RET_C == 0
    n_chunks = seq_len // RET_C
    hps = heads_per_step
    hb = hd // hps
    return pl.pallas_call(
        functools.partial(_ret_kernel, n_chunks=n_chunks, heads_per_step=hps),
        grid=(batch, hb),
        in_specs=[
            pl.BlockSpec(memory_space=pltpu.SMEM),
            pl.BlockSpec((seq_len, hps * dk), lambda b, h: (b, h)),
            pl.BlockSpec((seq_len, hps * dk), lambda b, h: (b, hb + h)),
            pl.BlockSpec((seq_len, hps * dv), lambda b, h: (b, h)),
            pl.BlockSpec((seq_len, hps * dv), lambda b, h: (b, h)),
            pl.BlockSpec((ctx_len, hps * dk), lambda b, h: (b, h)),
            pl.BlockSpec((ctx_len, hps * dv), lambda b, h: (b, hb + h)),
        ],
        out_specs=pl.BlockSpec((seq_len, hps * dv), lambda b, h: (b, h)),
        out_shape=jax.ShapeDtypeStruct((m, hd * dv), BF16),
        scratch_shapes=[
            pltpu.VMEM((hps, dk, dv), F32),
            pltpu.VMEM((hps, n_chunks, dk, dv), BF16),
        ],
        compiler_params=_params("arbitrary", "arbitrary"),
        name="retention",
    )(lg, qk, qk, v, g, kvc, kvc)


def _branch_kernel(ret_ref, u_ref, vs_ref, gr_ref, gs_ref, lng_ref, lnb_ref, sgw_ref, sgb_ref,
                   *rest, tm):
    o_ref, sgo_ref = rest[-2:]
    n_slabs = (len(rest) - 2) // 2
    wr_refs, ws_refs = rest[:n_slabs], rest[n_slabs:2 * n_slabs]
    width = vs_ref.shape[1]
    gd = width // SG_GROUPS
    for r0 in range(0, tm, SUB_ROWS):
        rows = slice(r0, r0 + SUB_ROWS)
        vs = vs_ref[rows, :].astype(F32)
        mu = jnp.mean(vs, axis=-1, keepdims=True)
        cen = vs - mu
        var = jnp.mean(cen * cen, axis=-1, keepdims=True)
        vn = ((cen * lax.rsqrt(var + EPS)) * lng_ref[...] + lnb_ref[...]).astype(BF16)
        for c0 in range(0, SUB_ROWS, SG_CHUNK):
            crow = slice(r0 + c0, r0 + c0 + SG_CHUNK)
            for gi in range(SG_GROUPS):
                cols = slice(gi * gd, (gi + 1) * gd)
                mixed = _dot(sgw_ref[gi].astype(BF16), vn[c0:c0 + SG_CHUNK, cols]) + sgb_ref[:, cols]
                sgo_ref[crow, cols] = (u_ref[crow, cols].astype(F32) * mixed).astype(BF16)
        ret = ret_ref[rows, :]
        sgo = sgo_ref[rows, :]
        for t in range(n_slabs):
            cols = slice(t * WEIGHT_COLS, (t + 1) * WEIGHT_COLS)
            yr = _dot(ret, wr_refs[t][...])
            ys = _dot(sgo, ws_refs[t][...])
            o_ref[rows, cols] = (gr_ref[rows, cols].astype(F32) * yr
                                 + gs_ref[rows, cols].astype(F32) * ys).astype(BF16)


def _branches(retg, uvs, gates, ln_g, ln_b, sgw, sgb_tab, wr, ws, tm):
    m, w = retg.shape
    d = wr.shape[1]
    row = lambda i: (i, 0)
    row1 = lambda i: (i, 1)
    const2 = lambda i: (0, 0)
    wr_specs, wr_args = _weight_slabs(wr, 0, d)
    ws_specs, ws_args = _weight_slabs(ws, 0, d)
    return pl.pallas_call(
        functools.partial(_branch_kernel, tm=tm),
        grid=(m // tm,),
        in_specs=[
            pl.BlockSpec((tm, w), row),
            pl.BlockSpec((tm, w), row),
            pl.BlockSpec((tm, w), row1),
            pl.BlockSpec((tm, d), row),
            pl.BlockSpec((tm, d), row1),
            _resident((1, w), const2),
            _resident((1, w), const2),
            _resident(sgw.shape, lambda i: (0, 0, 0)),
            _resident(sgb_tab.shape, const2),
        ] + wr_specs + ws_specs,
        out_specs=pl.BlockSpec((tm, d), row),
        out_shape=jax.ShapeDtypeStruct((m, d), BF16),
        scratch_shapes=[pltpu.VMEM((tm, w), BF16)],
        compiler_params=_params("arbitrary"),
        name="branches",
    )(retg, uvs, uvs, gates, gates, ln_g.reshape(1, w), ln_b.reshape(1, w), sgw, sgb_tab,
      *wr_args, *ws_args)


def _outproj_kernel(m_ref, x_ref, gt1_ref, g2_ref, sh2_ref, sc2_ref, *rest, tm):
    x1_ref, h2_ref = rest[-2:]
    wo_refs = rest[:-2]
    for r0 in range(0, tm, SUB_ROWS):
        rows = slice(r0, r0 + SUB_ROWS)
        mr = m_ref[rows, :]
        for t, wo_ref in enumerate(wo_refs):
            cols = slice(t * WEIGHT_COLS, (t + 1) * WEIGHT_COLS)
            x1_ref[rows, cols] = x_ref[rows, cols] + gt1_ref[0, :, cols] * _dot(mr, wo_ref[...])
        h2_ref[rows, :] = _modulated_norm(x1_ref[rows, :], g2_ref[...], sh2_ref[0], sc2_ref[0])


def _outproj(merged, wo, x2d, mod3, g2, seq_len, tm):
    m, d = x2d.shape
    per = seq_len // tm
    row = lambda i: (i, 0)
    modspec = lambda blk: pl.BlockSpec((1, 1, d), lambda i: (i // per, 0, blk))
    wo_specs, wo_args = _weight_slabs(wo, 0, d)
    return pl.pallas_call(
        functools.partial(_outproj_kernel, tm=tm),
        grid=(m // tm,),
        in_specs=[
            pl.BlockSpec((tm, d), row),
            pl.BlockSpec((tm, d), row),
            modspec(2), pl.BlockSpec((1, d), lambda i: (0, 0)), modspec(3), modspec(4),
        ] + wo_specs,
        out_specs=[pl.BlockSpec((tm, d), row), pl.BlockSpec((tm, d), row)],
        out_shape=[jax.ShapeDtypeStruct((m, d), F32), jax.ShapeDtypeStruct((m, d), BF16)],
        compiler_params=_params("arbitrary"),
        name="outproj",
    )(merged, x2d, mod3, g2.reshape(1, d), mod3, mod3, *wo_args)


def _ffn_up_kernel(h_ref, wa_ref, wb_ref, o_ref, *, tm):
    for r0 in range(0, tm, SUB_ROWS):
        rows = slice(r0, r0 + SUB_ROWS)
        hr = h_ref[rows, :]
        a = _dot(hr, wa_ref[...])
        b = _dot(hr, wb_ref[...])
        o_ref[rows, :] = (_silu(a) * b).astype(BF16)


def _ffn_up(h2, w_in, tm, tn):
    m, d = h2.shape
    hidden = w_in.shape[1] // 2
    nt = hidden // tn
    return pl.pallas_call(
        functools.partial(_ffn_up_kernel, tm=tm),
        grid=(m // tm, nt),
        in_specs=[
            pl.BlockSpec((tm, d), lambda i, j: (i, 0)),
            pl.BlockSpec((d, tn), lambda i, j: (0, j)),
            pl.BlockSpec((d, tn), lambda i, j: (0, nt + j)),
        ],
        out_specs=pl.BlockSpec((tm, tn), lambda i, j: (i, j)),
        out_shape=jax.ShapeDtypeStruct((m, hidden), BF16),
        compiler_params=_params("arbitrary", "arbitrary"),
        name="ffn_up",
    )(h2, w_in, w_in)


def _ffn_down_kernel(a_ref, x1_ref, gt2_ref, gf_ref, *rest, tm):
    o_ref = rest[-1]
    w_refs = rest[:-1]
    for r0 in range(0, tm, SUB_ROWS):
        rows = slice(r0, r0 + SUB_ROWS)
        ar = a_ref[rows, :]
        for t, w_ref in enumerate(w_refs):
            cols = slice(t * WEIGHT_COLS, (t + 1) * WEIGHT_COLS)
            o_ref[rows, cols] = x1_ref[rows, cols] + gt2_ref[0, :, cols] * _dot(ar, w_ref[...])
        o_ref[rows, :] = _rms(o_ref[rows, :]) * gf_ref[...]


def _ffn_down(act, w_out, x1, mod3, gf, seq_len, tm):
    m, d = x1.shape
    hidden = act.shape[1]
    per = seq_len // tm
    row = lambda i: (i, 0)
    w_specs, w_args = _weight_slabs(w_out, 0, d)
    return pl.pallas_call(
        functools.partial(_ffn_down_kernel, tm=tm),
        grid=(m // tm,),
        in_specs=[
            pl.BlockSpec((tm, hidden), row),
            pl.BlockSpec((tm, d), row),
            pl.BlockSpec((1, 1, d), lambda i: (i // per, 0, 5)),
            pl.BlockSpec((1, d), lambda i: (0, 0)),
        ] + w_specs,
        out_specs=pl.BlockSpec((tm, d), row),
        out_shape=jax.ShapeDtypeStruct((m, d), F32),
        compiler_params=_params("arbitrary"),
        name="ffn_down",
    )(act, x1, mod3, gf.reshape(1, d), *w_args)


def _rope_tables(seq_len, dk):
    freqs = dk // 4
    rows = seq_len // GRID_W
    row = np.repeat(np.arange(rows), GRID_W)
    col = np.tile(np.arange(GRID_W), rows)
    freq = ROPE_BASE ** (-np.arange(freqs, dtype=np.float64) / freqs)
    ang = np.stack([row, col], axis=-1).astype(np.float64)[:, :, None] * freq
    cos, sin = np.cos(ang), np.sin(ang)
    cos_t = np.concatenate([cos[:, 0], cos[:, 0], cos[:, 1], cos[:, 1]], axis=-1)
    sin_t = np.concatenate([-sin[:, 0], sin[:, 0], -sin[:, 1], sin[:, 1]], axis=-1)
    return jnp.asarray(cos_t, F32), jnp.asarray(sin_t, F32)


def kernel(x, c, ctx, c_ctx, w_mod, b_mod, norm1_g, w_in, ret_decay_fwd, ret_decay_bwd,
           sg_ln_g, sg_ln_b, sg_w, sg_b, w_ret_o, w_sg_o, w_out, norm2_g, w_ffn_in, w_ffn_out,
           final_norm_g):
    batch, seq_len, d = x.shape
    ctx_len = ctx.shape[1]
    depth = w_mod.shape[0]
    assert depth == 1
    width = w_ret_o.shape[1]
    dk = width // RET_HEADS
    q_off, k_off, v_off, g_off, u_off, gr_off = (i * width for i in (0, 1, 2, 3, 4, 6))

    x2d = x.reshape(batch * seq_len, d)
    ctx2d = ctx.reshape(batch * ctx_len, d)

    pad = (-(batch + 1)) % 8
    cond = jnp.concatenate([c, c_ctx[None], jnp.zeros((pad, d), F32)], axis=0)
    mod = _adaln(cond, w_mod[0], b_mod[0])
    mod3 = mod.reshape(mod.shape[0], 1, mod.shape[1])

    lg = jnp.stack([-jax.nn.softplus(-ret_decay_fwd[0].astype(F32)),
                    -jax.nn.softplus(-ret_decay_bwd[0].astype(F32))])

    hc = _prenorm(ctx2d, norm1_g[0], mod3, ctx_len, lambda b: batch, 0, 1, tl=ctx_len)
    w_qkv_b, kvc = _castproj(hc, w_in[0], g_off, k_off, 2 * width, width, dk ** -0.5)

    rope = _rope_tables(seq_len, dk)
    qk, h, w_g_b, w_uvs_b, w_gates_b = _proj(
        x2d, w_qkv_b, q_off, 2 * width, "rope", tm=512, rope=rope, seq_len=seq_len,
        col_scale=(width, 2 * width, dk ** -0.5), norm=(norm1_g[0], mod3, 0, 1),
        sides=((w_in[0], g_off, width), (w_in[0], u_off, 2 * width), (w_in[0], gr_off, 2 * d)),
        name="proj_qk")
    v, wr_b, ws_b, wo_b = _proj(h, w_qkv_b, v_off, width, "none", tm=1024,
                                sides=(w_ret_o[0], w_sg_o[0], w_out[0]), name="proj_v")
    g, w_down_b = _proj(h, w_g_b, 0, width, "silu", tm=1024, sides=(w_ffn_out[0],),
                        name="proj_g")
    uvs = _proj(h, w_uvs_b, 0, 2 * width, "gelu", tm=512, name="proj_uvs")
    gates, w_up_b = _proj(h, w_gates_b, 0, 2 * d, "sigmoid", tm=512, sides=(w_ffn_in[0],),
                          name="proj_gates")

    retg = _retention(lg, qk, v, g, kvc, batch, seq_len, ctx_len)

    sgb_tab = jnp.repeat(sg_b[0].T, width // SG_GROUPS, axis=1)
    merged = _branches(retg, uvs, gates, sg_ln_g[0], sg_ln_b[0], sg_w[0], sgb_tab, wr_b, ws_b,
                       tm=512)
    x1, h2 = _outproj(merged, wo_b, x2d, mod3, norm2_g[0], seq_len, tm=512)

    act = _ffn_up(h2, w_up_b, tm=4096, tn=512)
    out = _ffn_down(act, w_down_b, x1, mod3, final_norm_g, seq_len, tm=512)
    return out.reshape(batch, seq_len, d)
```

```python
import functools
import math

import jax
import jax.numpy as jnp
import numpy as np
from jax import lax
from jax.experimental import pallas as pl
from jax.experimental.pallas import tpu as pltpu

F32 = jnp.float32
BF16 = jnp.bfloat16

EPS = 1e-6
GRID_W = 64
ROPE_BASE = 10000.0
RET_HEADS = 8
SG_GROUPS = 8
SG_CHUNK = 128
V7X_VMEM_BYTES = 64 * 1024 * 1024
VMEM_LIMIT = V7X_VMEM_BYTES - 8 * 1024 * 1024
MXU_N = 256
RET_C = MXU_N
SUB_ROWS = 256
WEIGHT_COLS = 512


def _params(*sem):
    return pltpu.CompilerParams(dimension_semantics=sem, vmem_limit_bytes=VMEM_LIMIT)


def _sigmoid(x):
    return 1.0 / (1.0 + jnp.exp(-x))


def _silu(x):
    return x * _sigmoid(x)


def _gelu_tanh(x):
    b = -2.0 * math.sqrt(2.0 / math.pi) * math.log2(math.e)
    a = b * 0.044715
    return x / (1.0 + jnp.exp2(x * (a * (x * x) + b)))


def _rms(x):
    return x * lax.rsqrt(jnp.mean(x * x, axis=-1, keepdims=True) + EPS)


def _dot(a, b):
    return jnp.dot(a, b, preferred_element_type=F32)


def _resident(shape, index_map):
    return pl.BlockSpec(shape, index_map, pipeline_mode=pl.Buffered(1))


def _weight_slabs(w, col0, ncols):
    assert col0 % WEIGHT_COLS == 0 and ncols % WEIGHT_COLS == 0
    n = ncols // WEIGHT_COLS
    specs = [_resident((w.shape[0], WEIGHT_COLS), lambda i, t=t: (0, col0 // WEIGHT_COLS + t))
             for t in range(n)]
    return specs, [w] * n


def _adaln_kernel(c_ref, w_ref, b_ref, o_ref):
    s = _silu(c_ref[...]).astype(BF16)
    o_ref[...] = _dot(s, w_ref[...].astype(BF16)) + b_ref[...]


def _adaln(cond, w_mod, b_mod, tn=1024):
    rows, d = cond.shape
    n = w_mod.shape[1]
    return pl.pallas_call(
        _adaln_kernel,
        grid=(n // tn,),
        in_specs=[
            pl.BlockSpec((rows, d), lambda j: (0, 0)),
            pl.BlockSpec((d, tn), lambda j: (0, j)),
            pl.BlockSpec((1, tn), lambda j: (0, j)),
        ],
        out_specs=pl.BlockSpec((rows, tn), lambda j: (0, j)),
        out_shape=jax.ShapeDtypeStruct((rows, n), F32),
        compiler_params=_params("arbitrary"),
        name="adaln",
    )(cond, w_mod, b_mod.reshape(1, n))


def _modulated_norm(x, gain, sh):
    return (_rms(x) * gain + sh).astype(BF16)


def _prenorm_kernel(x_ref, g_ref, sh_ref, sc_ref, o_ref):
    o_ref[...] = _modulated_norm(x_ref[...], g_ref[...] * (1.0 + sc_ref[0]), sh_ref[0])


def _prenorm(x2d, g, mod3, rows_per_batch, mod_row, shift_blk, scale_blk, tl):
    m, d = x2d.shape
    per = rows_per_batch // tl
    return pl.pallas_call(
        _prenorm_kernel,
        grid=(m // tl,),
        in_specs=[
            pl.BlockSpec((tl, d), lambda i: (i, 0)),
            pl.BlockSpec((1, d), lambda i: (0, 0)),
            pl.BlockSpec((1, 1, d), lambda i: (mod_row(i // per), 0, shift_blk)),
            pl.BlockSpec((1, 1, d), lambda i: (mod_row(i // per), 0, scale_blk)),
        ],
        out_specs=pl.BlockSpec((tl, d), lambda i: (i, 0)),
        out_shape=jax.ShapeDtypeStruct((m, d), BF16),
        compiler_params=_params("arbitrary"),
        name="prenorm",
    )(x2d, g.reshape(1, d), mod3, mod3)


def _activate(acc, act):
    if act == "silu":
        return _silu(acc)
    if act == "gelu":
        return _gelu_tanh(acc)
    if act == "sigmoid":
        return _sigmoid(acc)
    return acc


def _proj_kernel(*refs, act, col_scale, tm, ncols, fused_norm, n_side):
    refs = list(refs)
    side_out = [refs.pop() for _ in range(n_side)][::-1]
    if fused_norm:
        x_ref, g_ref, sh_ref, sc_ref = refs[:4]
        refs = refs[4:]
        h_ref = refs.pop()
    else:
        h_ref = refs.pop(0)
    w_refs = [refs.pop(0) for _ in range(ncols // WEIGHT_COLS)]
    o_ref = refs.pop()
    side_in = [refs.pop() for _ in range(n_side)][::-1]
    if act == "rope":
        cos_ref, sin_ref = refs
    for src, dst in zip(side_in, side_out):
        dst[...] = src[...].astype(BF16)
    half = MXU_N // 2
    if fused_norm:
        gain = g_ref[...] * (1.0 + sc_ref[0])
    for r0 in range(0, tm, SUB_ROWS):
        rows = slice(r0, r0 + SUB_ROWS)
        if fused_norm:
            h_ref[rows, :] = _modulated_norm(x_ref[rows, :], gain, sh_ref[0])
        hr = h_ref[rows, :]
        for n0 in range(0, ncols, WEIGHT_COLS):
            acc = _dot(hr, w_refs[n0 // WEIGHT_COLS][...])
            if col_scale is not None and col_scale[0] <= n0 < col_scale[1]:
                acc = acc * col_scale[2]
            acc = _activate(acc, act)
            if act == "rope":
                for s in range(WEIGHT_COLS // half):
                    lanes = slice((s % 2) * half, (s % 2 + 1) * half)
                    xa = acc[:, s * half:(s + 1) * half]
                    ya = (xa * cos_ref[rows, lanes]
                          + pltpu.roll(xa, half // 2, axis=1) * sin_ref[rows, lanes])
                    o_ref[rows, n0 + s * half:n0 + (s + 1) * half] = ya.astype(BF16)
            else:
                o_ref[rows, n0:n0 + WEIGHT_COLS] = acc.astype(BF16)


def _proj(h, w, col0, ncols, act, *, tm, rope=None, seq_len=None, col_scale=None,
          norm=None, sides=(), name="proj"):
    m, k = h.shape
    steps = m // tm
    in_specs, args = [], []
    if norm is not None:
        g, mod3, shift_blk, scale_blk = norm
        per = seq_len // tm
        in_specs += [
            pl.BlockSpec((tm, k), lambda i: (i, 0)),
            pl.BlockSpec((1, k), lambda i: (0, 0)),
            pl.BlockSpec((1, 1, k), lambda i: (i // per, 0, shift_blk)),
            pl.BlockSpec((1, 1, k), lambda i: (i // per, 0, scale_blk)),
        ]
        args += [h, g.reshape(1, k), mod3, mod3]
    else:
        in_specs.append(pl.BlockSpec((tm, k), lambda i: (i, 0)))
        args.append(h)
    w_specs, w_args = _weight_slabs(w, col0, ncols)
    in_specs += w_specs
    args += w_args
    if act == "rope":
        per_l = seq_len // tm
        spec = pl.BlockSpec((tm, MXU_N), lambda i: (i % per_l, 0))
        in_specs += [spec, spec]
        args += list(rope)
    out_specs = [pl.BlockSpec((tm, ncols), lambda i: (i, 0))]
    out_shape = [jax.ShapeDtypeStruct((m, ncols), BF16)]
    if norm is not None:
        out_specs.append(pl.BlockSpec((tm, k), lambda i: (i, 0)))
        out_shape.append(jax.ShapeDtypeStruct((m, k), BF16))
    for side in sides:
        arr, c0, c = side if isinstance(side, tuple) else (side, 0, side.shape[1])
        r = arr.shape[0]
        assert r % (steps * 16) == 0 and c0 % c == 0
        in_specs.append(pl.BlockSpec((r // steps, c), lambda i, cb=c0 // c: (i, cb)))
        args.append(arr)
        out_specs.append(pl.BlockSpec((r // steps, c), lambda i: (i, 0)))
        out_shape.append(jax.ShapeDtypeStruct((r, c), BF16))
    outs = pl.pallas_call(
        functools.partial(_proj_kernel, act=act, col_scale=col_scale, tm=tm, ncols=ncols,
                          fused_norm=norm is not None, n_side=len(sides)),
        grid=(steps,),
        in_specs=in_specs,
        out_specs=out_specs,
        out_shape=out_shape,
        compiler_params=_params("arbitrary"),
        name=name,
    )(*args)
    return outs[0] if len(outs) == 1 else outs


def _castproj_kernel(hc_ref, w_ref, wb_ref, kv_ref, *, m, lo, hi, k_blocks, scale):
    j = pl.program_id(0)
    wb_ref[...] = w_ref[...].astype(BF16)

    @pl.when(jnp.logical_and(j >= lo, j < hi))
    def _():
        mult = jnp.where(j < lo + k_blocks, scale, 1.0).astype(F32)
        for r0 in range(0, m, SUB_ROWS):
            rows = slice(r0, r0 + SUB_ROWS)
            kv_ref[rows, :] = (_dot(hc_ref[rows, :], wb_ref[...]) * mult).astype(BF16)


def _castproj(hc, w_f32, n, k_col0, kv_cols, k_cols, scale, tn=1024):
    m, k = hc.shape
    lo, hi = k_col0 // tn, (k_col0 + kv_cols) // tn
    assert hi * tn <= n
    return pl.pallas_call(
        functools.partial(_castproj_kernel, m=m, lo=lo, hi=hi, k_blocks=k_cols // tn, scale=scale),
        grid=(n // tn,),
        in_specs=[
            _resident((m, k), lambda j: (0, 0)),
            pl.BlockSpec((k, tn), lambda j: (0, j)),
        ],
        out_specs=[
            pl.BlockSpec((k, tn), lambda j: (0, j)),
            pl.BlockSpec((m, tn), lambda j: (0, jnp.clip(j - lo, 0, hi - lo - 1))),
        ],
        out_shape=[jax.ShapeDtypeStruct((k, n), BF16), jax.ShapeDtypeStruct((m, kv_cols), BF16)],
        compiler_params=_params("arbitrary"),
        name="castproj_ctx",
    )(hc, w_f32)


def _dot_t(a, b):
    return lax.dot_general(a, b, (((0,), (0,)), ((), ())), preferred_element_type=F32)


def _dot_nt(a, b):
    return lax.dot_general(a, b, (((1,), (1,)), ((), ())), preferred_element_type=F32)


def _ret_kernel(lg_ref, q_ref, k_ref, v_ref, g_ref, kc_ref, vc_ref, o_ref, sf_ref, sb_ref,
                *, n_chunks, heads_per_step):
    c = RET_C
    ri = lax.broadcasted_iota(jnp.int32, (c, c), 0).astype(F32)
    ci = lax.broadcasted_iota(jnp.int32, (c, c), 1).astype(F32)
    diff = ri - ci
    zero = jnp.zeros((1, c), F32)

    def kv(kn, vn, dec):
        return _dot_t(kn * dec, vn)

    for t in range(heads_per_step):
        hd = pl.program_id(1) * heads_per_step + t
        hc = slice(t * c, (t + 1) * c)
        lgf = lg_ref[0, hd]
        lgb = lg_ref[1, hd]
        dmask = jnp.where(diff >= 0.0,
                          jnp.exp(lgf * jnp.maximum(diff, 0.0)),
                          jnp.exp(lgb * jnp.maximum(-diff, 0.0))).astype(BF16)
        qdf = jnp.exp(lgf * (ri + 1.0)).astype(BF16)
        qdb = jnp.exp(lgb * (c - ri)).astype(BF16)
        kdf = jnp.exp(lgf * (c - 1.0 - ri)).astype(BF16)
        kdb = jnp.exp(lgb * ri).astype(BF16)
        cdf = jnp.exp(zero + lgf * c)
        cdb = jnp.exp(zero + lgb * c)

        kc = kc_ref[:, hc]
        vc = vc_ref[:, hc]
        sf_ref[t] = kv(kc, vc, kdf)
        sb = kv(kc, vc, kdb)
        for n in reversed(range(n_chunks)):
            rows = slice(n * c, (n + 1) * c)
            sb_ref[t, n] = sb.astype(BF16)
            if n > 0:
                sb = sb * cdb + kv(k_ref[rows, hc], v_ref[rows, hc], kdb)
        for n in range(n_chunks):
            rows = slice(n * c, (n + 1) * c)
            qn = q_ref[rows, hc]
            kn = k_ref[rows, hc]
            vn = v_ref[rows, hc]
            a = _dot_nt(qn, kn).astype(BF16) * dmask
            o = (_dot(a, vn)
                 + _dot(qn * qdf, sf_ref[t].astype(BF16))
                 + _dot(qn * qdb, sb_ref[t, n]))
            o_ref[rows, hc] = _rms(o).astype(BF16) * g_ref[rows, hc]
            if n + 1 < n_chunks:
                sf_ref[t] = sf_ref[t] * cdf + kv(kn, vn, kdf)


def _retention(lg, qk, v, g, kvc, batch, seq_len, ctx_len, heads_per_step=2):
    m = qk.shape[0]
    hd = RET_HEADS
    dk = qk.shape[1] // (2 * hd)
    dv = v.shape[1] // hd
    assert dk == RET_C and dv == RET_C and ctx_len == RET_C and seq_len % RET_C == 0
    n_chunks = seq_len // RET_C
    hps = heads_per_step
    hb = hd // hps
    return pl.pallas_call(
        functools.partial(_ret_kernel, n_chunks=n_chunks, heads_per_step=hps),
        grid=(batch, hb),
        in_specs=[
            pl.BlockSpec(memory_space=pltpu.SMEM),
            pl.BlockSpec((seq_len, hps * dk), lambda b, h: (b, h)),
            pl.BlockSpec((seq_len, hps * dk), lambda b, h: (b, hb + h)),
            pl.BlockSpec((seq_len, hps * dv), lambda b, h: (b, h)),
            pl.BlockSpec((seq_len, hps * dv), lambda b, h: (b, h)),
            pl.BlockSpec((ctx_len, hps * dk), lambda b, h: (b, h)),
            pl.BlockSpec((ctx_len, hps * dv), lambda b, h: (b, hb + h)),
        ],
        out_specs=pl.BlockSpec((seq_len, hps * dv), lambda b, h: (b, h)),
        out_shape=jax.ShapeDtypeStruct((m, hd * dv), BF16),
        scratch_shapes=[
            pltpu.VMEM((hps, dk, dv), F32),
            pltpu.VMEM((hps, n_chunks, dk, dv), BF16),
        ],
        compiler_params=_params("arbitrary", "arbitrary"),
        name="retention",
    )(lg, qk, qk, v, g, kvc, kvc)


def _branch_kernel(ret_ref, u_ref, vs_ref, gr_ref, gs_ref, lng_ref, lnb_ref, sgw_ref, sgb_ref,
                   *rest, tm):
    o_ref, sgo_ref = rest[-2:]
    n_slabs = (len(rest) - 2) // 2
    wr_refs, ws_refs = rest[:n_slabs], rest[n_slabs:2 * n_slabs]
    width = vs_ref.shape[1]
    gd = width // SG_GROUPS
    for r0 in range(0, tm, SUB_ROWS):
        rows = slice(r0, r0 + SUB_ROWS)
        vs = vs_ref[rows, :].astype(F32)
        mu = jnp.mean(vs, axis=-1, keepdims=True)
        cen = vs - mu
        var = jnp.mean(cen * cen, axis=-1, keepdims=True)
        vn = ((cen * lax.rsqrt(var + EPS)) * lng_ref[...] + lnb_ref[...]).astype(BF16)
        for c0 in range(0, SUB_ROWS, SG_CHUNK):
            crow = slice(r0 + c0, r0 + c0 + SG_CHUNK)
            for gi in range(SG_GROUPS):
                cols = slice(gi * gd, (gi + 1) * gd)
                mixed = _dot(sgw_ref[gi].astype(BF16), vn[c0:c0 + SG_CHUNK, cols]) + sgb_ref[:, cols]
                sgo_ref[crow, cols] = u_ref[crow, cols] * mixed.astype(BF16)
        ret = ret_ref[rows, :]
        sgo = sgo_ref[rows, :]
        for t in range(n_slabs):
            cols = slice(t * WEIGHT_COLS, (t + 1) * WEIGHT_COLS)
            yr = _dot(ret, wr_refs[t][...])
            ys = _dot(sgo, ws_refs[t][...])
            o_ref[rows, cols] = (gr_ref[rows, cols] * yr.astype(BF16)
                                 + gs_ref[rows, cols] * ys.astype(BF16))


def _branches(retg, uvs, gates, ln_g, ln_b, sgw, sgb_tab, wr, ws, tm):
    m, w = retg.shape
    d = wr.shape[1]
    row = lambda i: (i, 0)
    row1 = lambda i: (i, 1)
    const2 = lambda i: (0, 0)
    wr_specs, wr_args = _weight_slabs(wr, 0, d)
    ws_specs, ws_args = _weight_slabs(ws, 0, d)
    return pl.pallas_call(
        functools.partial(_branch_kernel, tm=tm),
        grid=(m // tm,),
        in_specs=[
            pl.BlockSpec((tm, w), row),
            pl.BlockSpec((tm, w), row),
            pl.BlockSpec((tm, w), row1),
            pl.BlockSpec((tm, d), row),
            pl.BlockSpec((tm, d), row1),
            _resident((1, w), const2),
            _resident((1, w), const2),
            _resident(sgw.shape, lambda i: (0, 0, 0)),
            _resident(sgb_tab.shape, const2),
        ] + wr_specs + ws_specs,
        out_specs=pl.BlockSpec((tm, d), row),
        out_shape=jax.ShapeDtypeStruct((m, d), BF16),
        scratch_shapes=[pltpu.VMEM((tm, w), BF16)],
        compiler_params=_params("arbitrary"),
        name="branches",
    )(retg, uvs, uvs, gates, gates, ln_g.reshape(1, w), ln_b.reshape(1, w), sgw, sgb_tab,
      *wr_args, *ws_args)


def _outproj_kernel(m_ref, x_ref, gt1_ref, g2_ref, sh2_ref, sc2_ref, *rest, tm):
    x1_ref, h2_ref = rest[-2:]
    wo_refs = rest[:-2]
    gain = g2_ref[...] * (1.0 + sc2_ref[0])
    for r0 in range(0, tm, SUB_ROWS):
        rows = slice(r0, r0 + SUB_ROWS)
        mr = m_ref[rows, :]
        for t, wo_ref in enumerate(wo_refs):
            cols = slice(t * WEIGHT_COLS, (t + 1) * WEIGHT_COLS)
            x1_ref[rows, cols] = x_ref[rows, cols] + gt1_ref[0, :, cols] * _dot(mr, wo_ref[...])
        h2_ref[rows, :] = _modulated_norm(x1_ref[rows, :], gain, sh2_ref[0])


def _outproj(merged, wo, x2d, mod3, g2, seq_len, tm):
    m, d = x2d.shape
    per = seq_len // tm
    row = lambda i: (i, 0)
    modspec = lambda blk: pl.BlockSpec((1, 1, d), lambda i: (i // per, 0, blk))
    wo_specs, wo_args = _weight_slabs(wo, 0, d)
    return pl.pallas_call(
        functools.partial(_outproj_kernel, tm=tm),
        grid=(m // tm,),
        in_specs=[
            pl.BlockSpec((tm, d), row),
            pl.BlockSpec((tm, d), row),
            modspec(2), pl.BlockSpec((1, d), lambda i: (0, 0)), modspec(3), modspec(4),
        ] + wo_specs,
        out_specs=[pl.BlockSpec((tm, d), row), pl.BlockSpec((tm, d), row)],
        out_shape=[jax.ShapeDtypeStruct((m, d), F32), jax.ShapeDtypeStruct((m, d), BF16)],
        compiler_params=_params("arbitrary"),
        name="outproj",
    )(merged, x2d, mod3, g2.reshape(1, d), mod3, mod3, *wo_args)


def _ffn_up_kernel(h_ref, wa_ref, wb_ref, o_ref, *, tm):
    for r0 in range(0, tm, SUB_ROWS):
        rows = slice(r0, r0 + SUB_ROWS)
        hr = h_ref[rows, :]
        a = _dot(hr, wa_ref[...])
        b = _dot(hr, wb_ref[...])
        o_ref[rows, :] = (_silu(a) * b).astype(BF16)


def _ffn_up(h2, w_in, tm, tn):
    m, d = h2.shape
    hidden = w_in.shape[1] // 2
    nt = hidden // tn
    return pl.pallas_call(
        functools.partial(_ffn_up_kernel, tm=tm),
        grid=(m // tm, nt),
        in_specs=[
            pl.BlockSpec((tm, d), lambda i, j: (i, 0)),
            pl.BlockSpec((d, tn), lambda i, j: (0, j)),
            pl.BlockSpec((d, tn), lambda i, j: (0, nt + j)),
        ],
        out_specs=pl.BlockSpec((tm, tn), lambda i, j: (i, j)),
        out_shape=jax.ShapeDtypeStruct((m, hidden), BF16),
        compiler_params=_params("arbitrary", "arbitrary"),
        name="ffn_up",
    )(h2, w_in, w_in)


def _ffn_down_kernel(a_ref, x1_ref, gt2_ref, gf_ref, *rest, tm):
    o_ref = rest[-1]
    w_refs = rest[:-1]
    for r0 in range(0, tm, SUB_ROWS):
        rows = slice(r0, r0 + SUB_ROWS)
        ar = a_ref[rows, :]
        for t, w_ref in enumerate(w_refs):
            cols = slice(t * WEIGHT_COLS, (t + 1) * WEIGHT_COLS)
            o_ref[rows, cols] = x1_ref[rows, cols] + gt2_ref[0, :, cols] * _dot(ar, w_ref[...])
        o_ref[rows, :] = _rms(o_ref[rows, :]) * gf_ref[...]


def _ffn_down(act, w_out, x1, mod3, gf, seq_len, tm):
    m, d = x1.shape
    hidden = act.shape[1]
    per = seq_len // tm
    row = lambda i: (i, 0)
    w_specs, w_args = _weight_slabs(w_out, 0, d)
    return pl.pallas_call(
        functools.partial(_ffn_down_kernel, tm=tm),
        grid=(m // tm,),
        in_specs=[
            pl.BlockSpec((tm, hidden), row),
            pl.BlockSpec((tm, d), row),
            pl.BlockSpec((1, 1, d), lambda i: (i // per, 0, 5)),
            pl.BlockSpec((1, d), lambda i: (0, 0)),
        ] + w_specs,
        out_specs=pl.BlockSpec((tm, d), row),
        out_shape=jax.ShapeDtypeStruct((m, d), F32),
        compiler_params=_params("arbitrary"),
        name="ffn_down",
    )(act, x1, mod3, gf.reshape(1, d), *w_args)


def _rope_tables(seq_len, dk):
    freqs = dk // 4
    rows = seq_len // GRID_W
    row = np.repeat(np.arange(rows), GRID_W)
    col = np.tile(np.arange(GRID_W), rows)
    freq = ROPE_BASE ** (-np.arange(freqs, dtype=np.float64) / freqs)
    ang = np.stack([row, col], axis=-1).astype(np.float64)[:, :, None] * freq
    cos, sin = np.cos(ang), np.sin(ang)
    cos_t = np.concatenate([cos[:, 0], cos[:, 0], cos[:, 1], cos[:, 1]], axis=-1)
    sin_t = np.concatenate([-sin[:, 0], sin[:, 0], -sin[:, 1], sin[:, 1]], axis=-1)
    return jnp.asarray(cos_t, F32), jnp.asarray(sin_t, F32)


def kernel(x, c, ctx, c_ctx, w_mod, b_mod, norm1_g, w_in, ret_decay_fwd, ret_decay_bwd,
           sg_ln_g, sg_ln_b, sg_w, sg_b, w_ret_o, w_sg_o, w_out, norm2_g, w_ffn_in, w_ffn_out,
           final_norm_g):
    batch, seq_len, d = x.shape
    ctx_len = ctx.shape[1]
    depth = w_mod.shape[0]
    assert depth == 1
    width = w_ret_o.shape[1]
    dk = width // RET_HEADS
    q_off, k_off, v_off, g_off, u_off, gr_off = (i * width for i in (0, 1, 2, 3, 4, 6))

    x2d = x.reshape(batch * seq_len, d)
    ctx2d = ctx.reshape(batch * ctx_len, d)

    pad = (-(batch + 1)) % 8
    cond = jnp.concatenate([c, c_ctx[None], jnp.zeros((pad, d), F32)], axis=0)
    mod = _adaln(cond, w_mod[0], b_mod[0])
    mod3 = mod.reshape(mod.shape[0], 1, mod.shape[1])

    lg = jnp.stack([-jax.nn.softplus(-ret_decay_fwd[0].astype(F32)),
                    -jax.nn.softplus(-ret_decay_bwd[0].astype(F32))])

    hc = _prenorm(ctx2d, norm1_g[0], mod3, ctx_len, lambda b: batch, 0, 1, tl=ctx_len)
    w_qkv_b, kvc = _castproj(hc, w_in[0], g_off, k_off, 2 * width, width, dk ** -0.5)

    rope = _rope_tables(seq_len, dk)
    qk, h, w_g_b, w_uvs_b, w_gates_b = _proj(
        x2d, w_qkv_b, q_off, 2 * width, "rope", tm=512, rope=rope, seq_len=seq_len,
        col_scale=(width, 2 * width, dk ** -0.5), norm=(norm1_g[0], mod3, 0, 1),
        sides=((w_in[0], g_off, width), (w_in[0], u_off, 2 * width), (w_in[0], gr_off, 2 * d)),
        name="proj_qk")
    v, wr_b, ws_b, wo_b = _proj(h, w_qkv_b, v_off, width, "none", tm=1024,
                                sides=(w_ret_o[0], w_sg_o[0], w_out[0]), name="proj_v")
    g, w_down_b = _proj(h, w_g_b, 0, width, "silu", tm=1024, sides=(w_ffn_out[0],),
                        name="proj_g")
    uvs = _proj(h, w_uvs_b, 0, 2 * width, "gelu", tm=1024, name="proj_uvs")
    gates, w_up_b = _proj(h, w_gates_b, 0, 2 * d, "sigmoid", tm=512, sides=(w_ffn_in[0],),
                          name="proj_gates")

    retg = _retention(lg, qk, v, g, kvc, batch, seq_len, ctx_len)

    sgb_tab = jnp.repeat(sg_b[0].T, width // SG_GROUPS, axis=1)
    merged = _branches(retg, uvs, gates, sg_ln_g[0], sg_ln_b[0], sg_w[0], sgb_tab, wr_b, ws_b,
                       tm=512)
    x1, h2 = _outproj(merged, wo_b, x2d, mod3, norm2_g[0], seq_len, tm=512)

    act = _ffn_up(h2, w_up_b, tm=4096, tn=512)
    out = _ffn_down(act, w_down_b, x1, mod3, final_norm_g, seq_len, tm=512)
    return out.reshape(batch, seq_len, d)
```

```python
import functools
import math

import jax
import jax.numpy as jnp
import numpy as np
from jax import lax
from jax.experimental import pallas as pl
from jax.experimental.pallas import tpu as pltpu

F32 = jnp.float32
BF16 = jnp.bfloat16

EPS = 1e-6
GRID_W = 64
ROPE_BASE = 10000.0
RET_HEADS = 8
SG_GROUPS = 8
SG_CHUNK = 128
V7X_VMEM_BYTES = 64 * 1024 * 1024
VMEM_LIMIT = V7X_VMEM_BYTES - 8 * 1024 * 1024
MXU_N = 256
RET_C = MXU_N
SUB_ROWS = 256
WEIGHT_COLS = 512


def _params(*sem):
    return pltpu.CompilerParams(dimension_semantics=sem, vmem_limit_bytes=VMEM_LIMIT)


def _sigmoid(x):
    return 1.0 / (1.0 + jnp.exp(-x))


def _silu(x):
    return x * _sigmoid(x)


def _gelu_tanh(x):
    b = -2.0 * math.sqrt(2.0 / math.pi) * math.log2(math.e)
    a = b * 0.044715
    return x / (1.0 + jnp.exp2(x * (a * (x * x) + b)))


def _rms(x):
    return x * lax.rsqrt(jnp.mean(x * x, axis=-1, keepdims=True) + EPS)


def _dot(a, b):
    return jnp.dot(a, b, preferred_element_type=F32)


def _resident(shape, index_map):
    return pl.BlockSpec(shape, index_map, pipeline_mode=pl.Buffered(1))


def _weight_slabs(w, col0, ncols):
    assert col0 % WEIGHT_COLS == 0 and ncols % WEIGHT_COLS == 0
    n = ncols // WEIGHT_COLS
    specs = [_resident((w.shape[0], WEIGHT_COLS), lambda i, t=t: (0, col0 // WEIGHT_COLS + t))
             for t in range(n)]
    return specs, [w] * n


def _adaln_kernel(c_ref, w_ref, b_ref, o_ref):
    s = _silu(c_ref[...]).astype(BF16)
    o_ref[...] = _dot(s, w_ref[...].astype(BF16)) + b_ref[...]


def _adaln(cond, w_mod, b_mod2d, ncols, tn=1024):
    rows, d = cond.shape
    return pl.pallas_call(
        _adaln_kernel,
        grid=(ncols // tn,),
        in_specs=[
            pl.BlockSpec((rows, d), lambda j: (0, 0)),
            pl.BlockSpec((d, tn), lambda j: (0, j)),
            pl.BlockSpec((1, tn), lambda j: (0, j)),
        ],
        out_specs=pl.BlockSpec((rows, tn), lambda j: (0, j)),
        out_shape=jax.ShapeDtypeStruct((rows, ncols), F32),
        compiler_params=_params("arbitrary"),
        name="adaln",
    )(cond, w_mod, b_mod2d)


def _modulated_norm(x, gain, sh):
    return (_rms(x) * gain + sh).astype(BF16)


def _prenorm_kernel(x_ref, g_ref, sh_ref, sc_ref, o_ref):
    o_ref[...] = _modulated_norm(x_ref[...], g_ref[...] * (1.0 + sc_ref[0]), sh_ref[0])


def _prenorm(x2d, g, mod3, rows_per_batch, mod_row, shift_blk, scale_blk, tl):
    m, d = x2d.shape
    per = rows_per_batch // tl
    return pl.pallas_call(
        _prenorm_kernel,
        grid=(m // tl,),
        in_specs=[
            pl.BlockSpec((tl, d), lambda i: (i, 0)),
            pl.BlockSpec((1, d), lambda i: (0, 0)),
            pl.BlockSpec((1, 1, d), lambda i: (mod_row(i // per), 0, shift_blk)),
            pl.BlockSpec((1, 1, d), lambda i: (mod_row(i // per), 0, scale_blk)),
        ],
        out_specs=pl.BlockSpec((tl, d), lambda i: (i, 0)),
        out_shape=jax.ShapeDtypeStruct((m, d), BF16),
        compiler_params=_params("arbitrary"),
        name="prenorm",
    )(x2d, g.reshape(1, d), mod3, mod3)


def _activate(acc, act):
    if act == "silu":
        return _silu(acc)
    if act == "gelu":
        return _gelu_tanh(acc)
    if act == "sigmoid":
        return _sigmoid(acc)
    return acc


def _proj_kernel(*refs, act, col_scale, tm, ncols, fused_norm, n_side, mod_job):
    refs = list(refs)
    if mod_job:
        mo_ref = refs.pop()
    side_out = [refs.pop() for _ in range(n_side)][::-1]
    if fused_norm:
        x_ref, g_ref, sh_ref, sc_ref = refs[:4]
        refs = refs[4:]
        h_ref = refs.pop()
    else:
        h_ref = refs.pop(0)
    w_refs = [refs.pop(0) for _ in range(ncols // WEIGHT_COLS)]
    o_ref = refs.pop()
    if mod_job:
        c_ref, wm_ref, bm_ref = refs[-3:]
        refs = refs[:-3]
        _adaln_kernel(c_ref, wm_ref, bm_ref, mo_ref)
    side_in = [refs.pop() for _ in range(n_side)][::-1]
    if act == "rope":
        cos_ref, sin_ref = refs
    for src, dst in zip(side_in, side_out):
        dst[...] = src[...].astype(BF16)
    half = MXU_N // 2
    if fused_norm:
        gain = g_ref[...] * (1.0 + sc_ref[0])
    for r0 in range(0, tm, SUB_ROWS):
        rows = slice(r0, r0 + SUB_ROWS)
        if fused_norm:
            h_ref[rows, :] = _modulated_norm(x_ref[rows, :], gain, sh_ref[0])
        hr = h_ref[rows, :]
        for n0 in range(0, ncols, WEIGHT_COLS):
            acc = _dot(hr, w_refs[n0 // WEIGHT_COLS][...])
            if col_scale is not None and col_scale[0] <= n0 < col_scale[1]:
                acc = acc * col_scale[2]
            acc = _activate(acc, act)
            if act == "rope":
                for s in range(WEIGHT_COLS // half):
                    lanes = slice((s % 2) * half, (s % 2 + 1) * half)
                    xa = acc[:, s * half:(s + 1) * half]
                    ya = (xa * cos_ref[rows, lanes]
                          + pltpu.roll(xa, half // 2, axis=1) * sin_ref[rows, lanes])
                    o_ref[rows, n0 + s * half:n0 + (s + 1) * half] = ya.astype(BF16)
            else:
                o_ref[rows, n0:n0 + WEIGHT_COLS] = acc.astype(BF16)


def _proj(h, w, col0, ncols, act, *, tm, rope=None, seq_len=None, col_scale=None,
          norm=None, sides=(), mod_job=None, name="proj"):
    m, k = h.shape
    steps = m // tm
    in_specs, args = [], []
    if norm is not None:
        g, mod3, shift_blk, scale_blk = norm
        per = seq_len // tm
        in_specs += [
            pl.BlockSpec((tm, k), lambda i: (i, 0)),
            pl.BlockSpec((1, k), lambda i: (0, 0)),
            pl.BlockSpec((1, 1, k), lambda i: (i // per, 0, shift_blk)),
            pl.BlockSpec((1, 1, k), lambda i: (i // per, 0, scale_blk)),
        ]
        args += [h, g.reshape(1, k), mod3, mod3]
    else:
        in_specs.append(pl.BlockSpec((tm, k), lambda i: (i, 0)))
        args.append(h)
    w_specs, w_args = _weight_slabs(w, col0, ncols)
    in_specs += w_specs
    args += w_args
    if act == "rope":
        per_l = seq_len // tm
        spec = pl.BlockSpec((tm, MXU_N), lambda i: (i % per_l, 0))
        in_specs += [spec, spec]
        args += list(rope)
    out_specs = [pl.BlockSpec((tm, ncols), lambda i: (i, 0))]
    out_shape = [jax.ShapeDtypeStruct((m, ncols), BF16)]
    if norm is not None:
        out_specs.append(pl.BlockSpec((tm, k), lambda i: (i, 0)))
        out_shape.append(jax.ShapeDtypeStruct((m, k), BF16))
    for side in sides:
        arr, c0, c = side if isinstance(side, tuple) else (side, 0, side.shape[1])
        r = arr.shape[0]
        assert r % (steps * 16) == 0 and c0 % c == 0
        in_specs.append(pl.BlockSpec((r // steps, c), lambda i, cb=c0 // c: (i, cb)))
        args.append(arr)
        out_specs.append(pl.BlockSpec((r // steps, c), lambda i: (i, 0)))
        out_shape.append(jax.ShapeDtypeStruct((r, c), BF16))
    if mod_job is not None:
        cond, w_mod, b_mod2d, c0 = mod_job
        mc = (w_mod.shape[1] - c0) // steps
        assert mc * steps == w_mod.shape[1] - c0 and c0 % mc == 0
        in_specs += [
            pl.BlockSpec(cond.shape, lambda i: (0, 0)),
            pl.BlockSpec((w_mod.shape[0], mc), lambda i: (0, c0 // mc + i)),
            pl.BlockSpec((1, mc), lambda i: (0, c0 // mc + i)),
        ]
        args += [cond, w_mod, b_mod2d]
        out_specs.append(pl.BlockSpec((cond.shape[0], mc), lambda i: (0, i)))
        out_shape.append(jax.ShapeDtypeStruct((cond.shape[0], mc * steps), F32))
    outs = pl.pallas_call(
        functools.partial(_proj_kernel, act=act, col_scale=col_scale, tm=tm, ncols=ncols,
                          fused_norm=norm is not None, n_side=len(sides),
                          mod_job=mod_job is not None),
        grid=(steps,),
        in_specs=in_specs,
        out_specs=out_specs,
        out_shape=out_shape,
        compiler_params=_params("arbitrary"),
        name=name,
    )(*args)
    return outs[0] if len(outs) == 1 else outs


def _castproj_kernel(hc_ref, w_ref, wb_ref, kv_ref, *, m, lo, hi, k_blocks, scale):
    j = pl.program_id(0)
    wb_ref[...] = w_ref[...].astype(BF16)

    @pl.when(jnp.logical_and(j >= lo, j < hi))
    def _():
        mult = jnp.where(j < lo + k_blocks, scale, 1.0).astype(F32)
        for r0 in range(0, m, SUB_ROWS):
            rows = slice(r0, r0 + SUB_ROWS)
            kv_ref[rows, :] = (_dot(hc_ref[rows, :], wb_ref[...]) * mult).astype(BF16)


def _castproj(hc, w_f32, n, k_col0, kv_cols, k_cols, scale, tn=1024):
    m, k = hc.shape
    lo, hi = k_col0 // tn, (k_col0 + kv_cols) // tn
    assert hi * tn <= n
    return pl.pallas_call(
        functools.partial(_castproj_kernel, m=m, lo=lo, hi=hi, k_blocks=k_cols // tn, scale=scale),
        grid=(n // tn,),
        in_specs=[
            _resident((m, k), lambda j: (0, 0)),
            pl.BlockSpec((k, tn), lambda j: (0, j)),
        ],
        out_specs=[
            pl.BlockSpec((k, tn), lambda j: (0, j)),
            pl.BlockSpec((m, tn), lambda j: (0, jnp.clip(j - lo, 0, hi - lo - 1))),
        ],
        out_shape=[jax.ShapeDtypeStruct((k, n), BF16), jax.ShapeDtypeStruct((m, kv_cols), BF16)],
        compiler_params=_params("arbitrary"),
        name="castproj_ctx",
    )(hc, w_f32)


def _dot_t(a, b):
    return lax.dot_general(a, b, (((0,), (0,)), ((), ())), preferred_element_type=F32)


def _dot_nt(a, b):
    return lax.dot_general(a, b, (((1,), (1,)), ((), ())), preferred_element_type=F32)


def _ret_kernel(lg_ref, q_ref, k_ref, v_ref, g_ref, kc_ref, vc_ref, o_ref, sf_ref, sb_ref,
                *, n_chunks, heads_per_step):
    c = RET_C
    ri = lax.broadcasted_iota(jnp.int32, (c, c), 0).astype(F32)
    ci = lax.broadcasted_iota(jnp.int32, (c, c), 1).astype(F32)
    diff = ri - ci
    zero = jnp.zeros((1, c), F32)

    def kv(kn, vn, dec):
        return _dot_t(kn * dec, vn)

    for t in range(heads_per_step):
        hd = pl.program_id(1) * heads_per_step + t
        hc = slice(t * c, (t + 1) * c)
        lgf = lg_ref[0, hd]
        lgb = lg_ref[1, hd]
        dmask = jnp.where(diff >= 0.0,
                          jnp.exp(lgf * jnp.maximum(diff, 0.0)),
                          jnp.exp(lgb * jnp.maximum(-diff, 0.0))).astype(BF16)
        qdf = jnp.exp(lgf * (ri + 1.0)).astype(BF16)
        qdb = jnp.exp(lgb * (c - ri)).astype(BF16)
        kdf = jnp.exp(lgf * (c - 1.0 - ri)).astype(BF16)
        kdb = jnp.exp(lgb * ri).astype(BF16)
        cdf = jnp.exp(zero + lgf * c)
        cdb = jnp.exp(zero + lgb * c)

        kc = kc_ref[:, hc]
        vc = vc_ref[:, hc]
        sf_ref[t] = kv(kc, vc, kdf)
        sb = kv(kc, vc, kdb)
        for n in reversed(range(n_chunks)):
            rows = slice(n * c, (n + 1) * c)
            sb_ref[t, n] = sb.astype(BF16)
            if n > 0:
                sb = sb * cdb + kv(k_ref[rows, hc], v_ref[rows, hc], kdb)
        for n in range(n_chunks):
            rows = slice(n * c, (n + 1) * c)
            qn = q_ref[rows, hc]
            kn = k_ref[rows, hc]
            vn = v_ref[rows, hc]
            a = _dot_nt(qn, kn).astype(BF16) * dmask
            o = (_dot(a, vn)
                 + _dot(qn * qdf, sf_ref[t].astype(BF16))
                 + _dot(qn * qdb, sb_ref[t, n]))
            o_ref[rows, hc] = _rms(o).astype(BF16) * g_ref[rows, hc]
            if n + 1 < n_chunks:
                sf_ref[t] = sf_ref[t] * cdf + kv(kn, vn, kdf)


def _retention(lg, qk, v, g, kvc, batch, seq_len, ctx_len, heads_per_step=2):
    m = qk.shape[0]
    hd = RET_HEADS
    dk = qk.shape[1] // (2 * hd)
    dv = v.shape[1] // hd
    assert dk == RET_C and dv == RET_C and ctx_len == RET_C and seq_len % RET_C == 0
    n_chunks = seq_len // RET_C
    hps = heads_per_step
    hb = hd // hps
    return pl.pallas_call(
        functools.partial(_ret_kernel, n_chunks=n_chunks, heads_per_step=hps),
        grid=(batch, hb),
        in_specs=[
            pl.BlockSpec(memory_space=pltpu.SMEM),
            pl.BlockSpec((seq_len, hps * dk), lambda b, h: (b, h)),
            pl.BlockSpec((seq_len, hps * dk), lambda b, h: (b, hb + h)),
            pl.BlockSpec((seq_len, hps * dv), lambda b, h: (b, h)),
            pl.BlockSpec((seq_len, hps * dv), lambda b, h: (b, h)),
            pl.BlockSpec((ctx_len, hps * dk), lambda b, h: (b, h)),
            pl.BlockSpec((ctx_len, hps * dv), lambda b, h: (b, hb + h)),
        ],
        out_specs=pl.BlockSpec((seq_len, hps * dv), lambda b, h: (b, h)),
        out_shape=jax.ShapeDtypeStruct((m, hd * dv), BF16),
        scratch_shapes=[
            pltpu.VMEM((hps, dk, dv), F32),
            pltpu.VMEM((hps, n_chunks, dk, dv), BF16),
        ],
        compiler_params=_params("arbitrary", "arbitrary"),
        name="retention",
    )(lg, qk, qk, v, g, kvc, kvc)


def _branch_kernel(ret_ref, u_ref, vs_ref, gr_ref, gs_ref, lng_ref, lnb_ref, sgw_ref, sgb_ref,
                   *rest, tm):
    o_ref, sgo_ref = rest[-2:]
    n_slabs = (len(rest) - 2) // 2
    wr_refs, ws_refs = rest[:n_slabs], rest[n_slabs:2 * n_slabs]
    width = vs_ref.shape[1]
    gd = width // SG_GROUPS
    for r0 in range(0, tm, SUB_ROWS):
        rows = slice(r0, r0 + SUB_ROWS)
        vs = vs_ref[rows, :].astype(F32)
        mu = jnp.mean(vs, axis=-1, keepdims=True)
        cen = vs - mu
        var = jnp.mean(cen * cen, axis=-1, keepdims=True)
        vn = ((cen * lax.rsqrt(var + EPS)) * lng_ref[...] + lnb_ref[...]).astype(BF16)
        for c0 in range(0, SUB_ROWS, SG_CHUNK):
            crow = slice(r0 + c0, r0 + c0 + SG_CHUNK)
            for gi in range(SG_GROUPS):
                cols = slice(gi * gd, (gi + 1) * gd)
                mixed = _dot(sgw_ref[gi].astype(BF16), vn[c0:c0 + SG_CHUNK, cols]) + sgb_ref[:, cols]
                sgo_ref[crow, cols] = u_ref[crow, cols] * mixed.astype(BF16)
        ret = ret_ref[rows, :]
        sgo = sgo_ref[rows, :]
        for t in range(n_slabs):
            cols = slice(t * WEIGHT_COLS, (t + 1) * WEIGHT_COLS)
            yr = _dot(ret, wr_refs[t][...])
            ys = _dot(sgo, ws_refs[t][...])
            o_ref[rows, cols] = (gr_ref[rows, cols] * yr.astype(BF16)
                                 + gs_ref[rows, cols] * ys.astype(BF16))


def _branches(retg, uvs, gates, ln_g, ln_b, sgw, sgb_tab, wr, ws, tm):
    m, w = retg.shape
    d = wr.shape[1]
    row = lambda i: (i, 0)
    row1 = lambda i: (i, 1)
    const2 = lambda i: (0, 0)
    wr_specs, wr_args = _weight_slabs(wr, 0, d)
    ws_specs, ws_args = _weight_slabs(ws, 0, d)
    return pl.pallas_call(
        functools.partial(_branch_kernel, tm=tm),
        grid=(m // tm,),
        in_specs=[
            pl.BlockSpec((tm, w), row),
            pl.BlockSpec((tm, w), row),
            pl.BlockSpec((tm, w), row1),
            pl.BlockSpec((tm, d), row),
            pl.BlockSpec((tm, d), row1),
            _resident((1, w), const2),
            _resident((1, w), const2),
            _resident(sgw.shape, lambda i: (0, 0, 0)),
            _resident(sgb_tab.shape, const2),
        ] + wr_specs + ws_specs,
        out_specs=pl.BlockSpec((tm, d), row),
        out_shape=jax.ShapeDtypeStruct((m, d), BF16),
        scratch_shapes=[pltpu.VMEM((tm, w), BF16)],
        compiler_params=_params("arbitrary"),
        name="branches",
    )(retg, uvs, uvs, gates, gates, ln_g.reshape(1, w), ln_b.reshape(1, w), sgw, sgb_tab,
      *wr_args, *ws_args)


def _outproj_kernel(m_ref, x_ref, gt1_ref, g2_ref, sh2_ref, sc2_ref, *rest, tm):
    x1_ref, h2_ref = rest[-2:]
    wo_refs = rest[:-2]
    gain = g2_ref[...] * (1.0 + sc2_ref[0])
    for r0 in range(0, tm, SUB_ROWS):
        rows = slice(r0, r0 + SUB_ROWS)
        mr = m_ref[rows, :]
        for t, wo_ref in enumerate(wo_refs):
            cols = slice(t * WEIGHT_COLS, (t + 1) * WEIGHT_COLS)
            x1_ref[rows, cols] = x_ref[rows, cols] + gt1_ref[0, :, cols] * _dot(mr, wo_ref[...])
        h2_ref[rows, :] = _modulated_norm(x1_ref[rows, :], gain, sh2_ref[0])


def _outproj(merged, wo, x2d, mod3, mod_blks, g2, seq_len, tm):
    m, d = x2d.shape
    per = seq_len // tm
    row = lambda i: (i, 0)
    modspec = lambda blk: pl.BlockSpec((1, 1, d), lambda i: (i // per, 0, blk))
    wo_specs, wo_args = _weight_slabs(wo, 0, d)
    return pl.pallas_call(
        functools.partial(_outproj_kernel, tm=tm),
        grid=(m // tm,),
        in_specs=[
            pl.BlockSpec((tm, d), row),
            pl.BlockSpec((tm, d), row),
            modspec(mod_blks[0]), pl.BlockSpec((1, d), lambda i: (0, 0)),
            modspec(mod_blks[1]), modspec(mod_blks[2]),
        ] + wo_specs,
        out_specs=[pl.BlockSpec((tm, d), row), pl.BlockSpec((tm, d), row)],
        out_shape=[jax.ShapeDtypeStruct((m, d), F32), jax.ShapeDtypeStruct((m, d), BF16)],
        compiler_params=_params("arbitrary"),
        name="outproj",
    )(merged, x2d, mod3, g2.reshape(1, d), mod3, mod3, *wo_args)


def _ffn_up_kernel(h_ref, wa_ref, wb_ref, o_ref, *, tm):
    for r0 in range(0, tm, SUB_ROWS):
        rows = slice(r0, r0 + SUB_ROWS)
        hr = h_ref[rows, :]
        a = _dot(hr, wa_ref[...])
        b = _dot(hr, wb_ref[...])
        o_ref[rows, :] = (_silu(a) * b).astype(BF16)


def _ffn_up(h2, w_in, tm, tn):
    m, d = h2.shape
    hidden = w_in.shape[1] // 2
    nt = hidden // tn
    return pl.pallas_call(
        functools.partial(_ffn_up_kernel, tm=tm),
        grid=(m // tm, nt),
        in_specs=[
            pl.BlockSpec((tm, d), lambda i, j: (i, 0)),
            pl.BlockSpec((d, tn), lambda i, j: (0, j)),
            pl.BlockSpec((d, tn), lambda i, j: (0, nt + j)),
        ],
        out_specs=pl.BlockSpec((tm, tn), lambda i, j: (i, j)),
        out_shape=jax.ShapeDtypeStruct((m, hidden), BF16),
        compiler_params=_params("arbitrary", "arbitrary"),
        name="ffn_up",
    )(h2, w_in, w_in)


def _ffn_down_kernel(a_ref, x1_ref, gt2_ref, gf_ref, *rest, tm):
    o_ref = rest[-1]
    w_refs = rest[:-1]
    for r0 in range(0, tm, SUB_ROWS):
        rows = slice(r0, r0 + SUB_ROWS)
        ar = a_ref[rows, :]
        for t, w_ref in enumerate(w_refs):
            cols = slice(t * WEIGHT_COLS, (t + 1) * WEIGHT_COLS)
            o_ref[rows, cols] = x1_ref[rows, cols] + gt2_ref[0, :, cols] * _dot(ar, w_ref[...])
        o_ref[rows, :] = _rms(o_ref[rows, :]) * gf_ref[...]


def _ffn_down(act, w_out, x1, mod3, mod_blk, gf, seq_len, tm):
    m, d = x1.shape
    hidden = act.shape[1]
    per = seq_len // tm
    row = lambda i: (i, 0)
    w_specs, w_args = _weight_slabs(w_out, 0, d)
    return pl.pallas_call(
        functools.partial(_ffn_down_kernel, tm=tm),
        grid=(m // tm,),
        in_specs=[
            pl.BlockSpec((tm, hidden), row),
            pl.BlockSpec((tm, d), row),
            pl.BlockSpec((1, 1, d), lambda i: (i // per, 0, mod_blk)),
            pl.BlockSpec((1, d), lambda i: (0, 0)),
        ] + w_specs,
        out_specs=pl.BlockSpec((tm, d), row),
        out_shape=jax.ShapeDtypeStruct((m, d), F32),
        compiler_params=_params("arbitrary"),
        name="ffn_down",
    )(act, x1, mod3, gf.reshape(1, d), *w_args)


def _rope_tables(seq_len, dk):
    freqs = dk // 4
    rows = seq_len // GRID_W
    row = np.repeat(np.arange(rows), GRID_W)
    col = np.tile(np.arange(GRID_W), rows)
    freq = ROPE_BASE ** (-np.arange(freqs, dtype=np.float64) / freqs)
    ang = np.stack([row, col], axis=-1).astype(np.float64)[:, :, None] * freq
    cos, sin = np.cos(ang), np.sin(ang)
    cos_t = np.concatenate([cos[:, 0], cos[:, 0], cos[:, 1], cos[:, 1]], axis=-1)
    sin_t = np.concatenate([-sin[:, 0], sin[:, 0], -sin[:, 1], sin[:, 1]], axis=-1)
    return jnp.asarray(cos_t, F32), jnp.asarray(sin_t, F32)


def kernel(x, c, ctx, c_ctx, w_mod, b_mod, norm1_g, w_in, ret_decay_fwd, ret_decay_bwd,
           sg_ln_g, sg_ln_b, sg_w, sg_b, w_ret_o, w_sg_o, w_out, norm2_g, w_ffn_in, w_ffn_out,
           final_norm_g):
    batch, seq_len, d = x.shape
    ctx_len = ctx.shape[1]
    depth = w_mod.shape[0]
    assert depth == 1
    width = w_ret_o.shape[1]
    dk = width // RET_HEADS
    q_off, k_off, v_off, g_off, u_off, gr_off = (i * width for i in (0, 1, 2, 3, 4, 6))

    x2d = x.reshape(batch * seq_len, d)
    ctx2d = ctx.reshape(batch * ctx_len, d)

    pad = (-(batch + 1)) % 8
    cond = jnp.concatenate([c, c_ctx[None], jnp.zeros((pad, d), F32)], axis=0)
    b_mod2d = b_mod[0].reshape(1, -1)
    mod_a = _adaln(cond, w_mod[0], b_mod2d, 2 * d)
    mod_a3 = mod_a.reshape(mod_a.shape[0], 1, mod_a.shape[1])

    lg = jnp.stack([-jax.nn.softplus(-ret_decay_fwd[0].astype(F32)),
                    -jax.nn.softplus(-ret_decay_bwd[0].astype(F32))])

    hc = _prenorm(ctx2d, norm1_g[0], mod_a3, ctx_len, lambda b: batch, 0, 1, tl=ctx_len)
    w_qkv_b, kvc = _castproj(hc, w_in[0], g_off, k_off, 2 * width, width, dk ** -0.5)

    rope = _rope_tables(seq_len, dk)
    qk, h, w_g_b, w_uvs_b, w_gates_b = _proj(
        x2d, w_qkv_b, q_off, 2 * width, "rope", tm=512, rope=rope, seq_len=seq_len,
        col_scale=(width, 2 * width, dk ** -0.5), norm=(norm1_g[0], mod_a3, 0, 1),
        sides=((w_in[0], g_off, width), (w_in[0], u_off, 2 * width), (w_in[0], gr_off, 2 * d)),
        name="proj_qk")
    v, wr_b, ws_b, wo_b = _proj(h, w_qkv_b, v_off, width, "none", tm=1024,
                                sides=(w_ret_o[0], w_sg_o[0], w_out[0]), name="proj_v")
    g, w_down_b = _proj(h, w_g_b, 0, width, "silu", tm=1024, sides=(w_ffn_out[0],),
                        name="proj_g")
    uvs, mod_b = _proj(h, w_uvs_b, 0, 2 * width, "gelu", tm=1024,
                       mod_job=(cond, w_mod[0], b_mod2d, 2 * d), name="proj_uvs")
    mod_b3 = mod_b.reshape(mod_b.shape[0], 1, mod_b.shape[1])
    gates, w_up_b = _proj(h, w_gates_b, 0, 2 * d, "sigmoid", tm=512, sides=(w_ffn_in[0],),
                          name="proj_gates")

    retg = _retention(lg, qk, v, g, kvc, batch, seq_len, ctx_len)

    sgb_tab = jnp.repeat(sg_b[0].T, width // SG_GROUPS, axis=1)
    merged = _branches(retg, uvs, gates, sg_ln_g[0], sg_ln_b[0], sg_w[0], sgb_tab, wr_b, ws_b,
                       tm=512)
    x1, h2 = _outproj(merged, wo_b, x2d, mod_b3, (0, 1, 2), norm2_g[0], seq_len, tm=512)

    act = _ffn_up(h2, w_up_b, tm=4096, tn=512)
    out = _ffn_down(act, w_down_b, x1, mod_b3, 3, final_norm_g, seq_len, tm=512)
    return out.reshape(batch, seq_len, d)
```

```python
import functools
import math

import jax
import jax.numpy as jnp
import numpy as np
from jax import lax
from jax.experimental import pallas as pl
from jax.experimental.pallas import tpu as pltpu

F32 = jnp.float32
BF16 = jnp.bfloat16

EPS = 1e-6
GRID_W = 64
ROPE_BASE = 10000.0
RET_HEADS = 8
SG_GROUPS = 8
SG_CHUNK = 128
V7X_VMEM_BYTES = 64 * 1024 * 1024
VMEM_LIMIT = V7X_VMEM_BYTES - 8 * 1024 * 1024
MXU_N = 256
RET_C = MXU_N
SUB_ROWS = 256
WEIGHT_COLS = 512
ROW_TILE = dict(ctx_norm=1024, proj_qk=512, proj_v=1024, proj_g=1024, proj_uvs=1024,
                proj_gates=512, branches=512, outproj=512, ffn_up=4096, ffn_down=512)
FFN_UP_COLS = 512
RET_HEADS_PER_STEP = 4


def _params(*sem):
    return pltpu.CompilerParams(dimension_semantics=sem, vmem_limit_bytes=VMEM_LIMIT)


def _sigmoid(x):
    return 1.0 / (1.0 + jnp.exp(-x))


def _silu(x):
    return x * _sigmoid(x)


def _gelu_tanh(x):
    b = -2.0 * math.sqrt(2.0 / math.pi) * math.log2(math.e)
    a = b * 0.044715
    return x / (1.0 + jnp.exp2(x * (a * (x * x) + b)))


def _rms(x):
    return x * lax.rsqrt(jnp.mean(x * x, axis=-1, keepdims=True) + EPS)


def _dot(a, b):
    return jnp.dot(a, b, preferred_element_type=F32)


def _resident(shape, index_map):
    return pl.BlockSpec(shape, index_map, pipeline_mode=pl.Buffered(1))


def _weight_slabs(w, col0, ncols):
    assert col0 % WEIGHT_COLS == 0 and ncols % WEIGHT_COLS == 0
    n = ncols // WEIGHT_COLS
    specs = [_resident((w.shape[0], WEIGHT_COLS), lambda i, t=t: (0, col0 // WEIGHT_COLS + t))
             for t in range(n)]
    return specs, [w] * n


def _adaln_kernel(c_ref, w_ref, b_ref, o_ref):
    s = _silu(c_ref[...]).astype(BF16)
    o_ref[...] = _dot(s, w_ref[...].astype(BF16)) + b_ref[...]


def _adaln(cond, w_mod, b_mod2d, ncols, tn=1024):
    rows, d = cond.shape
    return pl.pallas_call(
        _adaln_kernel,
        grid=(ncols // tn,),
        in_specs=[
            pl.BlockSpec((rows, d), lambda j: (0, 0)),
            pl.BlockSpec((d, tn), lambda j: (0, j)),
            pl.BlockSpec((1, tn), lambda j: (0, j)),
        ],
        out_specs=pl.BlockSpec((rows, tn), lambda j: (0, j)),
        out_shape=jax.ShapeDtypeStruct((rows, ncols), F32),
        compiler_params=_params("arbitrary"),
        name="adaln",
    )(cond, w_mod, b_mod2d)


def _modulated_norm(x, gain, sh):
    return (_rms(x) * gain + sh).astype(BF16)


def _prenorm_kernel(x_ref, g_ref, sh_ref, sc_ref, o_ref):
    o_ref[...] = _modulated_norm(x_ref[...], g_ref[...] * (1.0 + sc_ref[0]), sh_ref[0])


def _prenorm(x2d, g, mod3, rows_per_batch, mod_row, shift_blk, scale_blk, tl):
    m, d = x2d.shape
    per = rows_per_batch // tl
    return pl.pallas_call(
        _prenorm_kernel,
        grid=(m // tl,),
        in_specs=[
            pl.BlockSpec((tl, d), lambda i: (i, 0)),
            pl.BlockSpec((1, d), lambda i: (0, 0)),
            pl.BlockSpec((1, 1, d), lambda i: (mod_row(i // per), 0, shift_blk)),
            pl.BlockSpec((1, 1, d), lambda i: (mod_row(i // per), 0, scale_blk)),
        ],
        out_specs=pl.BlockSpec((tl, d), lambda i: (i, 0)),
        out_shape=jax.ShapeDtypeStruct((m, d), BF16),
        compiler_params=_params("arbitrary"),
        name="prenorm",
    )(x2d, g.reshape(1, d), mod3, mod3)


def _activate(acc, act):
    if act == "silu":
        return _silu(acc)
    if act == "gelu":
        return _gelu_tanh(acc)
    if act == "sigmoid":
        return _sigmoid(acc)
    return acc


def _proj_kernel(*refs, act, col_scale, tm, ncols, fused_norm, n_side, mod_job):
    refs = list(refs)
    if mod_job:
        mo_ref = refs.pop()
    side_out = [refs.pop() for _ in range(n_side)][::-1]
    if fused_norm:
        x_ref, g_ref, sh_ref, sc_ref = refs[:4]
        refs = refs[4:]
        h_ref = refs.pop()
    else:
        h_ref = refs.pop(0)
    w_refs = [refs.pop(0) for _ in range(ncols // WEIGHT_COLS)]
    o_ref = refs.pop()
    if mod_job:
        c_ref, wm_ref, bm_ref = refs[-3:]
        refs = refs[:-3]
        _adaln_kernel(c_ref, wm_ref, bm_ref, mo_ref)
    side_in = [refs.pop() for _ in range(n_side)][::-1]
    if act == "rope":
        cos_ref, sin_ref = refs
    for src, dst in zip(side_in, side_out):
        dst[...] = src[...].astype(BF16)
    half = MXU_N // 2
    if fused_norm:
        gain = g_ref[...] * (1.0 + sc_ref[0])
    for r0 in range(0, tm, SUB_ROWS):
        rows = slice(r0, r0 + SUB_ROWS)
        if fused_norm:
            h_ref[rows, :] = _modulated_norm(x_ref[rows, :], gain, sh_ref[0])
        hr = h_ref[rows, :]
        for n0 in range(0, ncols, WEIGHT_COLS):
            acc = _dot(hr, w_refs[n0 // WEIGHT_COLS][...])
            if col_scale is not None and col_scale[0] <= n0 < col_scale[1]:
                acc = acc * col_scale[2]
            acc = _activate(acc, act)
            if act == "rope":
                for s in range(WEIGHT_COLS // half):
                    lanes = slice((s % 2) * half, (s % 2 + 1) * half)
                    xa = acc[:, s * half:(s + 1) * half]
                    ya = (xa * cos_ref[rows, lanes]
                          + pltpu.roll(xa, half // 2, axis=1) * sin_ref[rows, lanes])
                    o_ref[rows, n0 + s * half:n0 + (s + 1) * half] = ya.astype(BF16)
            else:
                o_ref[rows, n0:n0 + WEIGHT_COLS] = acc.astype(BF16)


def _proj(h, w, col0, ncols, act, *, tm, rope=None, seq_len=None, col_scale=None,
          norm=None, sides=(), mod_job=None, name="proj"):
    m, k = h.shape
    steps = m // tm
    in_specs, args = [], []
    if norm is not None:
        g, mod3, shift_blk, scale_blk = norm
        per = seq_len // tm
        in_specs += [
            pl.BlockSpec((tm, k), lambda i: (i, 0)),
            pl.BlockSpec((1, k), lambda i: (0, 0)),
            pl.BlockSpec((1, 1, k), lambda i: (i // per, 0, shift_blk)),
            pl.BlockSpec((1, 1, k), lambda i: (i // per, 0, scale_blk)),
        ]
        args += [h, g.reshape(1, k), mod3, mod3]
    else:
        in_specs.append(pl.BlockSpec((tm, k), lambda i: (i, 0)))
        args.append(h)
    w_specs, w_args = _weight_slabs(w, col0, ncols)
    in_specs += w_specs
    args += w_args
    if act == "rope":
        per_l = seq_len // tm
        spec = pl.BlockSpec((tm, MXU_N), lambda i: (i % per_l, 0))
        in_specs += [spec, spec]
        args += list(rope)
    out_specs = [pl.BlockSpec((tm, ncols), lambda i: (i, 0))]
    out_shape = [jax.ShapeDtypeStruct((m, ncols), BF16)]
    if norm is not None:
        out_specs.append(pl.BlockSpec((tm, k), lambda i: (i, 0)))
        out_shape.append(jax.ShapeDtypeStruct((m, k), BF16))
    for side in sides:
        arr, c0, c = side if isinstance(side, tuple) else (side, 0, side.shape[1])
        r = arr.shape[0]
        assert r % (steps * 16) == 0 and c0 % c == 0
        in_specs.append(pl.BlockSpec((r // steps, c), lambda i, cb=c0 // c: (i, cb)))
        args.append(arr)
        out_specs.append(pl.BlockSpec((r // steps, c), lambda i: (i, 0)))
        out_shape.append(jax.ShapeDtypeStruct((r, c), BF16))
    if mod_job is not None:
        cond, w_mod, b_mod2d, c0 = mod_job
        mc = (w_mod.shape[1] - c0) // steps
        assert mc * steps == w_mod.shape[1] - c0 and c0 % mc == 0
        in_specs += [
            pl.BlockSpec(cond.shape, lambda i: (0, 0)),
            pl.BlockSpec((w_mod.shape[0], mc), lambda i: (0, c0 // mc + i)),
            pl.BlockSpec((1, mc), lambda i: (0, c0 // mc + i)),
        ]
        args += [cond, w_mod, b_mod2d]
        out_specs.append(pl.BlockSpec((cond.shape[0], mc), lambda i: (0, i)))
        out_shape.append(jax.ShapeDtypeStruct((cond.shape[0], mc * steps), F32))
    outs = pl.pallas_call(
        functools.partial(_proj_kernel, act=act, col_scale=col_scale, tm=tm, ncols=ncols,
                          fused_norm=norm is not None, n_side=len(sides),
                          mod_job=mod_job is not None),
        grid=(steps,),
        in_specs=in_specs,
        out_specs=out_specs,
        out_shape=out_shape,
        compiler_params=_params("arbitrary"),
        name=name,
    )(*args)
    return outs[0] if len(outs) == 1 else outs


def _castproj_kernel(hc_ref, w_ref, wb_ref, kv_ref, *, m, lo, hi, k_blocks, scale):
    j = pl.program_id(0)
    wb_ref[...] = w_ref[...].astype(BF16)

    @pl.when(jnp.logical_and(j >= lo, j < hi))
    def _():
        mult = jnp.where(j < lo + k_blocks, scale, 1.0).astype(F32)
        for r0 in range(0, m, SUB_ROWS):
            rows = slice(r0, r0 + SUB_ROWS)
            kv_ref[rows, :] = (_dot(hc_ref[rows, :], wb_ref[...]) * mult).astype(BF16)


def _castproj(hc, w_f32, n, k_col0, kv_cols, k_cols, scale, tn=1024):
    m, k = hc.shape
    lo, hi = k_col0 // tn, (k_col0 + kv_cols) // tn
    assert hi * tn <= n
    return pl.pallas_call(
        functools.partial(_castproj_kernel, m=m, lo=lo, hi=hi, k_blocks=k_cols // tn, scale=scale),
        grid=(n // tn,),
        in_specs=[
            _resident((m, k), lambda j: (0, 0)),
            pl.BlockSpec((k, tn), lambda j: (0, j)),
        ],
        out_specs=[
            pl.BlockSpec((k, tn), lambda j: (0, j)),
            pl.BlockSpec((m, tn), lambda j: (0, jnp.clip(j - lo, 0, hi - lo - 1))),
        ],
        out_shape=[jax.ShapeDtypeStruct((k, n), BF16), jax.ShapeDtypeStruct((m, kv_cols), BF16)],
        compiler_params=_params("arbitrary"),
        name="castproj_ctx",
    )(hc, w_f32)


def _dot_t(a, b):
    return lax.dot_general(a, b, (((0,), (0,)), ((), ())), preferred_element_type=F32)


def _dot_nt(a, b):
    return lax.dot_general(a, b, (((1,), (1,)), ((), ())), preferred_element_type=F32)


def _ret_kernel(lg_ref, q_ref, k_ref, v_ref, g_ref, kc_ref, vc_ref, o_ref, sf_ref, sb_ref,
                *, n_chunks, heads_per_step):
    c = RET_C
    ri = lax.broadcasted_iota(jnp.int32, (c, c), 0).astype(F32)
    ci = lax.broadcasted_iota(jnp.int32, (c, c), 1).astype(F32)
    diff = ri - ci
    zero = jnp.zeros((1, c), F32)

    def kv(kn, vn, dec):
        return _dot_t(kn * dec, vn)

    for t in range(heads_per_step):
        hd = pl.program_id(1) * heads_per_step + t
        hc = slice(t * c, (t + 1) * c)
        lgf = lg_ref[0, hd]
        lgb = lg_ref[1, hd]
        dmask = jnp.where(diff >= 0.0,
                          jnp.exp(lgf * jnp.maximum(diff, 0.0)),
                          jnp.exp(lgb * jnp.maximum(-diff, 0.0))).astype(BF16)
        qdf = jnp.exp(lgf * (ri + 1.0)).astype(BF16)
        qdb = jnp.exp(lgb * (c - ri)).astype(BF16)
        kdf = jnp.exp(lgf * (c - 1.0 - ri)).astype(BF16)
        kdb = jnp.exp(lgb * ri).astype(BF16)
        cdf = jnp.exp(zero + lgf * c)
        cdb = jnp.exp(zero + lgb * c)

        kc = kc_ref[:, hc]
        vc = vc_ref[:, hc]
        sf_ref[t] = kv(kc, vc, kdf)
        sb = kv(kc, vc, kdb)
        for n in reversed(range(n_chunks)):
            rows = slice(n * c, (n + 1) * c)
            sb_ref[t, n] = sb.astype(BF16)
            if n > 0:
                sb = sb * cdb + kv(k_ref[rows, hc], v_ref[rows, hc], kdb)
        for n in range(n_chunks):
            rows = slice(n * c, (n + 1) * c)
            qn = q_ref[rows, hc]
            kn = k_ref[rows, hc]
            vn = v_ref[rows, hc]
            a = _dot_nt(qn, kn).astype(BF16) * dmask
            o = (_dot(a, vn)
                 + _dot(qn * qdf, sf_ref[t].astype(BF16))
                 + _dot(qn * qdb, sb_ref[t, n]))
            o_ref[rows, hc] = _rms(o).astype(BF16) * g_ref[rows, hc]
            if n + 1 < n_chunks:
                sf_ref[t] = sf_ref[t] * cdf + kv(kn, vn, kdf)


def _retention(lg, qk, v, g, kvc, batch, seq_len, ctx_len, heads_per_step=RET_HEADS_PER_STEP):
    m = qk.shape[0]
    hd = RET_HEADS
    dk = qk.shape[1] // (2 * hd)
    dv = v.shape[1] // hd
    assert dk == RET_C and dv == RET_C and ctx_len == RET_C and seq_len % RET_C == 0
    n_chunks = seq_len // RET_C
    hps = heads_per_step
    hb = hd // hps
    return pl.pallas_call(
        functools.partial(_ret_kernel, n_chunks=n_chunks, heads_per_step=hps),
        grid=(batch, hb),
        in_specs=[
            pl.BlockSpec(memory_space=pltpu.SMEM),
            pl.BlockSpec((seq_len, hps * dk), lambda b, h: (b, h)),
            pl.BlockSpec((seq_len, hps * dk), lambda b, h: (b, hb + h)),
            pl.BlockSpec((seq_len, hps * dv), lambda b, h: (b, h)),
            pl.BlockSpec((seq_len, hps * dv), lambda b, h: (b, h)),
            pl.BlockSpec((ctx_len, hps * dk), lambda b, h: (b, h)),
            pl.BlockSpec((ctx_len, hps * dv), lambda b, h: (b, hb + h)),
        ],
        out_specs=pl.BlockSpec((seq_len, hps * dv), lambda b, h: (b, h)),
        out_shape=jax.ShapeDtypeStruct((m, hd * dv), BF16),
        scratch_shapes=[
            pltpu.VMEM((hps, dk, dv), F32),
            pltpu.VMEM((hps, n_chunks, dk, dv), BF16),
        ],
        compiler_params=_params("arbitrary", "arbitrary"),
        name="retention",
    )(lg, qk, qk, v, g, kvc, kvc)


def _branch_kernel(ret_ref, u_ref, vs_ref, gr_ref, gs_ref, lng_ref, lnb_ref, sgw_ref, sgb_ref,
                   *rest, tm):
    o_ref, sgo_ref = rest[-2:]
    n_slabs = (len(rest) - 2) // 2
    wr_refs, ws_refs = rest[:n_slabs], rest[n_slabs:2 * n_slabs]
    width = vs_ref.shape[1]
    gd = width // SG_GROUPS
    for r0 in range(0, tm, SUB_ROWS):
        rows = slice(r0, r0 + SUB_ROWS)
        vs = vs_ref[rows, :].astype(F32)
        mu = jnp.mean(vs, axis=-1, keepdims=True)
        cen = vs - mu
        var = jnp.mean(cen * cen, axis=-1, keepdims=True)
        vn = ((cen * lax.rsqrt(var + EPS)) * lng_ref[...] + lnb_ref[...]).astype(BF16)
        for c0 in range(0, SUB_ROWS, SG_CHUNK):
            crow = slice(r0 + c0, r0 + c0 + SG_CHUNK)
            for gi in range(SG_GROUPS):
                cols = slice(gi * gd, (gi + 1) * gd)
                mixed = _dot(sgw_ref[gi].astype(BF16), vn[c0:c0 + SG_CHUNK, cols]) + sgb_ref[:, cols]
                sgo_ref[crow, cols] = u_ref[crow, cols] * mixed.astype(BF16)
        ret = ret_ref[rows, :]
        sgo = sgo_ref[rows, :]
        for t in range(n_slabs):
            cols = slice(t * WEIGHT_COLS, (t + 1) * WEIGHT_COLS)
            yr = _dot(ret, wr_refs[t][...])
            ys = _dot(sgo, ws_refs[t][...])
            o_ref[rows, cols] = (gr_ref[rows, cols] * yr.astype(BF16)
                                 + gs_ref[rows, cols] * ys.astype(BF16))


def _branches(retg, uvs, gates, ln_g, ln_b, sgw, sgb_tab, wr, ws, tm):
    m, w = retg.shape
    d = wr.shape[1]
    row = lambda i: (i, 0)
    row1 = lambda i: (i, 1)
    const2 = lambda i: (0, 0)
    wr_specs, wr_args = _weight_slabs(wr, 0, d)
    ws_specs, ws_args = _weight_slabs(ws, 0, d)
    return pl.pallas_call(
        functools.partial(_branch_kernel, tm=tm),
        grid=(m // tm,),
        in_specs=[
            pl.BlockSpec((tm, w), row),
            pl.BlockSpec((tm, w), row),
            pl.BlockSpec((tm, w), row1),
            pl.BlockSpec((tm, d), row),
            pl.BlockSpec((tm, d), row1),
            _resident((1, w), const2),
            _resident((1, w), const2),
            _resident(sgw.shape, lambda i: (0, 0, 0)),
            _resident(sgb_tab.shape, const2),
        ] + wr_specs + ws_specs,
        out_specs=pl.BlockSpec((tm, d), row),
        out_shape=jax.ShapeDtypeStruct((m, d), BF16),
        scratch_shapes=[pltpu.VMEM((tm, w), BF16)],
        compiler_params=_params("arbitrary"),
        name="branches",
    )(retg, uvs, uvs, gates, gates, ln_g.reshape(1, w), ln_b.reshape(1, w), sgw, sgb_tab,
      *wr_args, *ws_args)


def _outproj_kernel(m_ref, x_ref, gt1_ref, g2_ref, sh2_ref, sc2_ref, *rest, tm):
    x1_ref, h2_ref = rest[-2:]
    wo_refs = rest[:-2]
    gain = g2_ref[...] * (1.0 + sc2_ref[0])
    for r0 in range(0, tm, SUB_ROWS):
        rows = slice(r0, r0 + SUB_ROWS)
        mr = m_ref[rows, :]
        for t, wo_ref in enumerate(wo_refs):
            cols = slice(t * WEIGHT_COLS, (t + 1) * WEIGHT_COLS)
            x1_ref[rows, cols] = x_ref[rows, cols] + gt1_ref[0, :, cols] * _dot(mr, wo_ref[...])
        h2_ref[rows, :] = _modulated_norm(x1_ref[rows, :], gain, sh2_ref[0])


def _outproj(merged, wo, x2d, mod3, mod_blks, g2, seq_len, tm):
    m, d = x2d.shape
    per = seq_len // tm
    row = lambda i: (i, 0)
    modspec = lambda blk: pl.BlockSpec((1, 1, d), lambda i: (i // per, 0, blk))
    wo_specs, wo_args = _weight_slabs(wo, 0, d)
    return pl.pallas_call(
        functools.partial(_outproj_kernel, tm=tm),
        grid=(m // tm,),
        in_specs=[
            pl.BlockSpec((tm, d), row),
            pl.BlockSpec((tm, d), row),
            modspec(mod_blks[0]), pl.BlockSpec((1, d), lambda i: (0, 0)),
            modspec(mod_blks[1]), modspec(mod_blks[2]),
        ] + wo_specs,
        out_specs=[pl.BlockSpec((tm, d), row), pl.BlockSpec((tm, d), row)],
        out_shape=[jax.ShapeDtypeStruct((m, d), F32), jax.ShapeDtypeStruct((m, d), BF16)],
        compiler_params=_params("arbitrary"),
        name="outproj",
    )(merged, x2d, mod3, g2.reshape(1, d), mod3, mod3, *wo_args)


def _ffn_up_kernel(h_ref, wa_ref, wb_ref, o_ref, *, tm):
    for r0 in range(0, tm, SUB_ROWS):
        rows = slice(r0, r0 + SUB_ROWS)
        hr = h_ref[rows, :]
        a = _dot(hr, wa_ref[...])
        b = _dot(hr, wb_ref[...])
        o_ref[rows, :] = (_silu(a) * b).astype(BF16)


def _ffn_up(h2, w_in, tm, tn):
    m, d = h2.shape
    hidden = w_in.shape[1] // 2
    nt = hidden // tn
    return pl.pallas_call(
        functools.partial(_ffn_up_kernel, tm=tm),
        grid=(m // tm, nt),
        in_specs=[
            pl.BlockSpec((tm, d), lambda i, j: (i, 0)),
            pl.BlockSpec((d, tn), lambda i, j: (0, j)),
            pl.BlockSpec((d, tn), lambda i, j: (0, nt + j)),
        ],
        out_specs=pl.BlockSpec((tm, tn), lambda i, j: (i, j)),
        out_shape=jax.ShapeDtypeStruct((m, hidden), BF16),
        compiler_params=_params("arbitrary", "arbitrary"),
        name="ffn_up",
    )(h2, w_in, w_in)


def _ffn_down_kernel(a_ref, x1_ref, gt2_ref, gf_ref, *rest, tm):
    o_ref = rest[-1]
    w_refs = rest[:-1]
    for r0 in range(0, tm, SUB_ROWS):
        rows = slice(r0, r0 + SUB_ROWS)
        ar = a_ref[rows, :]
        for t, w_ref in enumerate(w_refs):
            cols = slice(t * WEIGHT_COLS, (t + 1) * WEIGHT_COLS)
            o_ref[rows, cols] = x1_ref[rows, cols] + gt2_ref[0, :, cols] * _dot(ar, w_ref[...])
        o_ref[rows, :] = _rms(o_ref[rows, :]) * gf_ref[...]


def _ffn_down(act, w_out, x1, mod3, mod_blk, gf, seq_len, tm):
    m, d = x1.shape
    hidden = act.shape[1]
    per = seq_len // tm
    row = lambda i: (i, 0)
    w_specs, w_args = _weight_slabs(w_out, 0, d)
    return pl.pallas_call(
        functools.partial(_ffn_down_kernel, tm=tm),
        grid=(m // tm,),
        in_specs=[
            pl.BlockSpec((tm, hidden), row),
            pl.BlockSpec((tm, d), row),
            pl.BlockSpec((1, 1, d), lambda i: (i // per, 0, mod_blk)),
            pl.BlockSpec((1, d), lambda i: (0, 0)),
        ] + w_specs,
        out_specs=pl.BlockSpec((tm, d), row),
        out_shape=jax.ShapeDtypeStruct((m, d), F32),
        compiler_params=_params("arbitrary"),
        name="ffn_down",
    )(act, x1, mod3, gf.reshape(1, d), *w_args)


def _rope_tables(seq_len, dk):
    freqs = dk // 4
    rows = seq_len // GRID_W
    row = np.repeat(np.arange(rows), GRID_W)
    col = np.tile(np.arange(GRID_W), rows)
    freq = ROPE_BASE ** (-np.arange(freqs, dtype=np.float64) / freqs)
    ang = np.stack([row, col], axis=-1).astype(np.float64)[:, :, None] * freq
    cos, sin = np.cos(ang), np.sin(ang)
    cos_t = np.concatenate([cos[:, 0], cos[:, 0], cos[:, 1], cos[:, 1]], axis=-1)
    sin_t = np.concatenate([-sin[:, 0], sin[:, 0], -sin[:, 1], sin[:, 1]], axis=-1)
    return jnp.asarray(cos_t, F32), jnp.asarray(sin_t, F32)


def kernel(x, c, ctx, c_ctx, w_mod, b_mod, norm1_g, w_in, ret_decay_fwd, ret_decay_bwd,
           sg_ln_g, sg_ln_b, sg_w, sg_b, w_ret_o, w_sg_o, w_out, norm2_g, w_ffn_in, w_ffn_out,
           final_norm_g):
    batch, seq_len, d = x.shape
    ctx_len = ctx.shape[1]
    depth = w_mod.shape[0]
    assert depth == 1
    width = w_ret_o.shape[1]
    dk = width // RET_HEADS
    q_off, k_off, v_off, g_off, u_off, gr_off = (i * width for i in (0, 1, 2, 3, 4, 6))

    x2d = x.reshape(batch * seq_len, d)
    ctx2d = ctx.reshape(batch * ctx_len, d)

    pad = (-(batch + 1)) % 8
    cond = jnp.concatenate([c, c_ctx[None], jnp.zeros((pad, d), F32)], axis=0)
    b_mod2d = b_mod[0].reshape(1, -1)
    mod_a = _adaln(cond, w_mod[0], b_mod2d, 2 * d)
    mod_a3 = mod_a.reshape(mod_a.shape[0], 1, mod_a.shape[1])

    lg = jnp.stack([-jax.nn.softplus(-ret_decay_fwd[0].astype(F32)),
                    -jax.nn.softplus(-ret_decay_bwd[0].astype(F32))])

    hc = _prenorm(ctx2d, norm1_g[0], mod_a3, ctx2d.shape[0], lambda b: batch, 0, 1,
                  tl=ROW_TILE["ctx_norm"])
    w_qkv_b, kvc = _castproj(hc, w_in[0], g_off, k_off, 2 * width, width, dk ** -0.5)

    rope = _rope_tables(seq_len, dk)
    qk, h, w_g_b, w_uvs_b, w_gates_b = _proj(
        x2d, w_qkv_b, q_off, 2 * width, "rope", tm=ROW_TILE["proj_qk"], rope=rope, seq_len=seq_len,
        col_scale=(width, 2 * width, dk ** -0.5), norm=(norm1_g[0], mod_a3, 0, 1),
        sides=((w_in[0], g_off, width), (w_in[0], u_off, 2 * width), (w_in[0], gr_off, 2 * d)),
        name="proj_qk")
    v, wr_b, ws_b, wo_b = _proj(h, w_qkv_b, v_off, width, "none", tm=ROW_TILE["proj_v"],
                                sides=(w_ret_o[0], w_sg_o[0], w_out[0]), name="proj_v")
    g, w_down_b = _proj(h, w_g_b, 0, width, "silu", tm=ROW_TILE["proj_g"],
                        sides=(w_ffn_out[0],), name="proj_g")
    uvs, mod_b = _proj(h, w_uvs_b, 0, 2 * width, "gelu", tm=ROW_TILE["proj_uvs"],
                       mod_job=(cond, w_mod[0], b_mod2d, 2 * d), name="proj_uvs")
    mod_b3 = mod_b.reshape(mod_b.shape[0], 1, mod_b.shape[1])
    gates, w_up_b = _proj(h, w_gates_b, 0, 2 * d, "sigmoid", tm=ROW_TILE["proj_gates"],
                          sides=(w_ffn_in[0],), name="proj_gates")

    retg = _retention(lg, qk, v, g, kvc, batch, seq_len, ctx_len)

    sgb_tab = jnp.repeat(sg_b[0].T, width // SG_GROUPS, axis=1)
    merged = _branches(retg, uvs, gates, sg_ln_g[0], sg_ln_b[0], sg_w[0], sgb_tab, wr_b, ws_b,
                       tm=ROW_TILE["branches"])
    x1, h2 = _outproj(merged, wo_b, x2d, mod_b3, (0, 1, 2), norm2_g[0], seq_len,
                      tm=ROW_TILE["outproj"])

    act = _ffn_up(h2, w_up_b, tm=ROW_TILE["ffn_up"], tn=FFN_UP_COLS)
    out = _ffn_down(act, w_down_b, x1, mod_b3, 3, final_norm_g, seq_len,
                    tm=ROW_TILE["ffn_down"])
    return out.reshape(batch, seq_len, d)
```

```python
import functools
import math

import jax
import jax.numpy as jnp
import numpy as np
from jax import lax
from jax.experimental import pallas as pl
from jax.experimental.pallas import tpu as pltpu

F32 = jnp.float32
BF16 = jnp.bfloat16

EPS = 1e-6
GRID_W = 64
ROPE_BASE = 10000.0
RET_HEADS = 8
SG_GROUPS = 8
SG_CHUNK = 128
V7X_VMEM_BYTES = 64 * 1024 * 1024
VMEM_LIMIT = V7X_VMEM_BYTES - 8 * 1024 * 1024
MXU_N = 256
RET_C = MXU_N
SUB_ROWS = 256
WEIGHT_COLS = 512
ROW_TILE = dict(ctx_norm=1024, proj_qk=512, proj_v=1024, proj_g=1024, proj_uvs=1024,
                proj_gates=512, branches=512, outproj=512, ffn_up=4096, ffn_down=512)
FFN_UP_COLS = 512
RET_HEADS_PER_STEP = 2


def _params(*sem):
    return pltpu.CompilerParams(dimension_semantics=sem, vmem_limit_bytes=VMEM_LIMIT)


def _sigmoid(x):
    return 1.0 / (1.0 + jnp.exp(-x))


def _silu(x):
    return x * _sigmoid(x)


def _gelu_tanh(x):
    b = -2.0 * math.sqrt(2.0 / math.pi) * math.log2(math.e)
    a = b * 0.044715
    return x / (1.0 + jnp.exp2(x * (a * (x * x) + b)))


def _rms(x):
    return x * lax.rsqrt(jnp.mean(x * x, axis=-1, keepdims=True) + EPS)


def _dot(a, b):
    return jnp.dot(a, b, preferred_element_type=F32)


def _resident(shape, index_map):
    return pl.BlockSpec(shape, index_map, pipeline_mode=pl.Buffered(1))


def _weight_slabs(w, col0, ncols):
    assert col0 % WEIGHT_COLS == 0 and ncols % WEIGHT_COLS == 0
    n = ncols // WEIGHT_COLS
    specs = [_resident((w.shape[0], WEIGHT_COLS), lambda i, t=t: (0, col0 // WEIGHT_COLS + t))
             for t in range(n)]
    return specs, [w] * n


def _adaln_kernel(c_ref, w_ref, b_ref, o_ref):
    s = _silu(c_ref[...]).astype(BF16)
    o_ref[...] = _dot(s, w_ref[...].astype(BF16)) + b_ref[...]


def _adaln(cond, w_mod, b_mod2d, ncols, tn=1024):
    rows, d = cond.shape
    return pl.pallas_call(
        _adaln_kernel,
        grid=(ncols // tn,),
        in_specs=[
            pl.BlockSpec((rows, d), lambda j: (0, 0)),
            pl.BlockSpec((d, tn), lambda j: (0, j)),
            pl.BlockSpec((1, tn), lambda j: (0, j)),
        ],
        out_specs=pl.BlockSpec((rows, tn), lambda j: (0, j)),
        out_shape=jax.ShapeDtypeStruct((rows, ncols), F32),
        compiler_params=_params("arbitrary"),
        name="adaln",
    )(cond, w_mod, b_mod2d)


def _modulated_norm(x, gain, sh):
    return (_rms(x) * gain + sh).astype(BF16)


def _prenorm_kernel(x_ref, g_ref, sh_ref, sc_ref, o_ref):
    o_ref[...] = _modulated_norm(x_ref[...], g_ref[...] * (1.0 + sc_ref[0]), sh_ref[0])


def _prenorm(x2d, g, mod3, rows_per_batch, mod_row, shift_blk, scale_blk, tl):
    m, d = x2d.shape
    per = rows_per_batch // tl
    return pl.pallas_call(
        _prenorm_kernel,
        grid=(m // tl,),
        in_specs=[
            pl.BlockSpec((tl, d), lambda i: (i, 0)),
            pl.BlockSpec((1, d), lambda i: (0, 0)),
            pl.BlockSpec((1, 1, d), lambda i: (mod_row(i // per), 0, shift_blk)),
            pl.BlockSpec((1, 1, d), lambda i: (mod_row(i // per), 0, scale_blk)),
        ],
        out_specs=pl.BlockSpec((tl, d), lambda i: (i, 0)),
        out_shape=jax.ShapeDtypeStruct((m, d), BF16),
        compiler_params=_params("arbitrary"),
        name="prenorm",
    )(x2d, g.reshape(1, d), mod3, mod3)


def _activate(acc, act):
    if act == "silu":
        return _silu(acc)
    if act == "gelu":
        return _gelu_tanh(acc)
    if act == "sigmoid":
        return _sigmoid(acc)
    return acc


def _proj_kernel(*refs, act, col_scale, tm, ncols, fused_norm, n_side, mod_job):
    refs = list(refs)
    if mod_job:
        mo_ref = refs.pop()
    side_out = [refs.pop() for _ in range(n_side)][::-1]
    if fused_norm:
        x_ref, g_ref, sh_ref, sc_ref = refs[:4]
        refs = refs[4:]
        h_ref = refs.pop()
    else:
        h_ref = refs.pop(0)
    w_refs = [refs.pop(0) for _ in range(ncols // WEIGHT_COLS)]
    o_ref = refs.pop()
    if mod_job:
        c_ref, wm_ref, bm_ref = refs[-3:]
        refs = refs[:-3]
        _adaln_kernel(c_ref, wm_ref, bm_ref, mo_ref)
    side_in = [refs.pop() for _ in range(n_side)][::-1]
    if act == "rope":
        cos_ref, sin_ref = refs
    for src, dst in zip(side_in, side_out):
        dst[...] = src[...].astype(BF16)
    half = MXU_N // 2
    if fused_norm:
        gain = g_ref[...] * (1.0 + sc_ref[0])
    for r0 in range(0, tm, SUB_ROWS):
        rows = slice(r0, r0 + SUB_ROWS)
        if fused_norm:
            h_ref[rows, :] = _modulated_norm(x_ref[rows, :], gain, sh_ref[0])
        hr = h_ref[rows, :]
        for n0 in range(0, ncols, WEIGHT_COLS):
            acc = _dot(hr, w_refs[n0 // WEIGHT_COLS][...])
            if col_scale is not None and col_scale[0] <= n0 < col_scale[1]:
                acc = acc * col_scale[2]
            acc = _activate(acc, act)
            if act == "rope":
                for s in range(WEIGHT_COLS // half):
                    lanes = slice((s % 2) * half, (s % 2 + 1) * half)
                    xa = acc[:, s * half:(s + 1) * half]
                    ya = (xa * cos_ref[rows, lanes]
                          + pltpu.roll(xa, half // 2, axis=1) * sin_ref[rows, lanes])
                    o_ref[rows, n0 + s * half:n0 + (s + 1) * half] = ya.astype(BF16)
            else:
                o_ref[rows, n0:n0 + WEIGHT_COLS] = acc.astype(BF16)


def _proj(h, w, col0, ncols, act, *, tm, rope=None, seq_len=None, col_scale=None,
          norm=None, sides=(), mod_job=None, name="proj"):
    m, k = h.shape
    steps = m // tm
    in_specs, args = [], []
    if norm is not None:
        g, mod3, shift_blk, scale_blk = norm
        per = seq_len // tm
        in_specs += [
            pl.BlockSpec((tm, k), lambda i: (i, 0)),
            pl.BlockSpec((1, k), lambda i: (0, 0)),
            pl.BlockSpec((1, 1, k), lambda i: (i // per, 0, shift_blk)),
            pl.BlockSpec((1, 1, k), lambda i: (i // per, 0, scale_blk)),
        ]
        args += [h, g.reshape(1, k), mod3, mod3]
    else:
        in_specs.append(pl.BlockSpec((tm, k), lambda i: (i, 0)))
        args.append(h)
    w_specs, w_args = _weight_slabs(w, col0, ncols)
    in_specs += w_specs
    args += w_args
    if act == "rope":
        per_l = seq_len // tm
        spec = pl.BlockSpec((tm, MXU_N), lambda i: (i % per_l, 0))
        in_specs += [spec, spec]
        args += list(rope)
    out_specs = [pl.BlockSpec((tm, ncols), lambda i: (i, 0))]
    out_shape = [jax.ShapeDtypeStruct((m, ncols), BF16)]
    if norm is not None:
        out_specs.append(pl.BlockSpec((tm, k), lambda i: (i, 0)))
        out_shape.append(jax.ShapeDtypeStruct((m, k), BF16))
    for side in sides:
        arr, c0, c = side if isinstance(side, tuple) else (side, 0, side.shape[1])
        r = arr.shape[0]
        assert r % (steps * 16) == 0 and c0 % c == 0
        in_specs.append(pl.BlockSpec((r // steps, c), lambda i, cb=c0 // c: (i, cb)))
        args.append(arr)
        out_specs.append(pl.BlockSpec((r // steps, c), lambda i: (i, 0)))
        out_shape.append(jax.ShapeDtypeStruct((r, c), BF16))
    if mod_job is not None:
        cond, w_mod, b_mod2d, c0 = mod_job
        mc = (w_mod.shape[1] - c0) // steps
        assert mc * steps == w_mod.shape[1] - c0 and c0 % mc == 0
        in_specs += [
            pl.BlockSpec(cond.shape, lambda i: (0, 0)),
            pl.BlockSpec((w_mod.shape[0], mc), lambda i: (0, c0 // mc + i)),
            pl.BlockSpec((1, mc), lambda i: (0, c0 // mc + i)),
        ]
        args += [cond, w_mod, b_mod2d]
        out_specs.append(pl.BlockSpec((cond.shape[0], mc), lambda i: (0, i)))
        out_shape.append(jax.ShapeDtypeStruct((cond.shape[0], mc * steps), F32))
    outs = pl.pallas_call(
        functools.partial(_proj_kernel, act=act, col_scale=col_scale, tm=tm, ncols=ncols,
                          fused_norm=norm is not None, n_side=len(sides),
                          mod_job=mod_job is not None),
        grid=(steps,),
        in_specs=in_specs,
        out_specs=out_specs,
        out_shape=out_shape,
        compiler_params=_params("arbitrary"),
        name=name,
    )(*args)
    return outs[0] if len(outs) == 1 else outs


def _castproj_kernel(hc_ref, w_ref, wb_ref, kv_ref, *, m, lo, hi, k_blocks, scale):
    j = pl.program_id(0)
    wb_ref[...] = w_ref[...].astype(BF16)

    @pl.when(jnp.logical_and(j >= lo, j < hi))
    def _():
        mult = jnp.where(j < lo + k_blocks, scale, 1.0).astype(F32)
        for r0 in range(0, m, SUB_ROWS):
            rows = slice(r0, r0 + SUB_ROWS)
            kv_ref[rows, :] = (_dot(hc_ref[rows, :], wb_ref[...]) * mult).astype(BF16)


def _castproj(hc, w_f32, n, k_col0, kv_cols, k_cols, scale, tn=1024):
    m, k = hc.shape
    lo, hi = k_col0 // tn, (k_col0 + kv_cols) // tn
    assert hi * tn <= n
    return pl.pallas_call(
        functools.partial(_castproj_kernel, m=m, lo=lo, hi=hi, k_blocks=k_cols // tn, scale=scale),
        grid=(n // tn,),
        in_specs=[
            _resident((m, k), lambda j: (0, 0)),
            pl.BlockSpec((k, tn), lambda j: (0, j)),
        ],
        out_specs=[
            pl.BlockSpec((k, tn), lambda j: (0, j)),
            pl.BlockSpec((m, tn), lambda j: (0, jnp.clip(j - lo, 0, hi - lo - 1))),
        ],
        out_shape=[jax.ShapeDtypeStruct((k, n), BF16), jax.ShapeDtypeStruct((m, kv_cols), BF16)],
        compiler_params=_params("arbitrary"),
        name="castproj_ctx",
    )(hc, w_f32)


def _dot_t(a, b):
    return lax.dot_general(a, b, (((0,), (0,)), ((), ())), preferred_element_type=F32)


def _dot_nt(a, b):
    return lax.dot_general(a, b, (((1,), (1,)), ((), ())), preferred_element_type=F32)


def _ret_kernel(lg_ref, q_ref, k_ref, v_ref, kc_ref, vc_ref, o_ref, sf_ref, sb_ref,
                *, n_chunks, heads_per_step):
    c = RET_C
    ri = lax.broadcasted_iota(jnp.int32, (c, c), 0).astype(F32)
    ci = lax.broadcasted_iota(jnp.int32, (c, c), 1).astype(F32)
    diff = ri - ci
    zero = jnp.zeros((1, c), F32)

    def kv(kn, vn, dec):
        return _dot_t(kn * dec, vn)

    for t in range(heads_per_step):
        hd = pl.program_id(1) * heads_per_step + t
        hc = slice(t * c, (t + 1) * c)
        lgf = lg_ref[0, hd]
        lgb = lg_ref[1, hd]
        dmask = jnp.where(diff >= 0.0,
                          jnp.exp(lgf * jnp.maximum(diff, 0.0)),
                          jnp.exp(lgb * jnp.maximum(-diff, 0.0))).astype(BF16)
        qdf = jnp.exp(lgf * (ri + 1.0)).astype(BF16)
        qdb = jnp.exp(lgb * (c - ri)).astype(BF16)
        kdf = jnp.exp(lgf * (c - 1.0 - ri)).astype(BF16)
        kdb = jnp.exp(lgb * ri).astype(BF16)
        cdf = jnp.exp(zero + lgf * c)
        cdb = jnp.exp(zero + lgb * c)

        kc = kc_ref[:, hc]
        vc = vc_ref[:, hc]
        sf_ref[t] = kv(kc, vc, kdf)
        sb = kv(kc, vc, kdb)
        for n in reversed(range(n_chunks)):
            rows = slice(n * c, (n + 1) * c)
            sb_ref[t, n] = sb.astype(BF16)
            if n > 0:
                sb = sb * cdb + kv(k_ref[rows, hc], v_ref[rows, hc], kdb)
        for n in range(n_chunks):
            rows = slice(n * c, (n + 1) * c)
            qn = q_ref[rows, hc]
            kn = k_ref[rows, hc]
            vn = v_ref[rows, hc]
            a = _dot_nt(qn, kn).astype(BF16) * dmask
            o = (_dot(a, vn)
                 + _dot(qn * qdf, sf_ref[t].astype(BF16))
                 + _dot(qn * qdb, sb_ref[t, n]))
            o_ref[rows, hc] = _rms(o).astype(BF16)
            if n + 1 < n_chunks:
                sf_ref[t] = sf_ref[t] * cdf + kv(kn, vn, kdf)


def _retention(lg, qk, v, kvc, batch, seq_len, ctx_len, heads_per_step=RET_HEADS_PER_STEP):
    m = qk.shape[0]
    hd = RET_HEADS
    dk = qk.shape[1] // (2 * hd)
    dv = v.shape[1] // hd
    assert dk == RET_C and dv == RET_C and ctx_len == RET_C and seq_len % RET_C == 0
    n_chunks = seq_len // RET_C
    hps = heads_per_step
    hb = hd // hps
    return pl.pallas_call(
        functools.partial(_ret_kernel, n_chunks=n_chunks, heads_per_step=hps),
        grid=(batch, hb),
        in_specs=[
            pl.BlockSpec(memory_space=pltpu.SMEM),
            pl.BlockSpec((seq_len, hps * dk), lambda b, h: (b, h)),
            pl.BlockSpec((seq_len, hps * dk), lambda b, h: (b, hb + h)),
            pl.BlockSpec((seq_len, hps * dv), lambda b, h: (b, h)),
            pl.BlockSpec((ctx_len, hps * dk), lambda b, h: (b, h)),
            pl.BlockSpec((ctx_len, hps * dv), lambda b, h: (b, hb + h)),
        ],
        out_specs=pl.BlockSpec((seq_len, hps * dv), lambda b, h: (b, h)),
        out_shape=jax.ShapeDtypeStruct((m, hd * dv), BF16),
        scratch_shapes=[
            pltpu.VMEM((hps, dk, dv), F32),
            pltpu.VMEM((hps, n_chunks, dk, dv), BF16),
        ],
        compiler_params=_params("arbitrary", "arbitrary"),
        name="retention",
    )(lg, qk, qk, v, kvc, kvc)


def _branch_kernel(ret_ref, sw_ref, u_ref, vs_ref, gr_ref, gs_ref, lng_ref, lnb_ref, sgw_ref,
                   sgb_ref, *rest, tm):
    o_ref, sgo_ref = rest[-2:]
    n_slabs = (len(rest) - 2) // 2
    wr_refs, ws_refs = rest[:n_slabs], rest[n_slabs:2 * n_slabs]
    width = vs_ref.shape[1]
    gd = width // SG_GROUPS
    for r0 in range(0, tm, SUB_ROWS):
        rows = slice(r0, r0 + SUB_ROWS)
        vs = vs_ref[rows, :].astype(F32)
        mu = jnp.mean(vs, axis=-1, keepdims=True)
        cen = vs - mu
        var = jnp.mean(cen * cen, axis=-1, keepdims=True)
        vn = ((cen * lax.rsqrt(var + EPS)) * lng_ref[...] + lnb_ref[...]).astype(BF16)
        for c0 in range(0, SUB_ROWS, SG_CHUNK):
            crow = slice(r0 + c0, r0 + c0 + SG_CHUNK)
            for gi in range(SG_GROUPS):
                cols = slice(gi * gd, (gi + 1) * gd)
                mixed = _dot(sgw_ref[gi].astype(BF16), vn[c0:c0 + SG_CHUNK, cols]) + sgb_ref[:, cols]
                sgo_ref[crow, cols] = u_ref[crow, cols] * mixed.astype(BF16)
        ret = ret_ref[rows, :] * sw_ref[rows, :]
        sgo = sgo_ref[rows, :]
        for t in range(n_slabs):
            cols = slice(t * WEIGHT_COLS, (t + 1) * WEIGHT_COLS)
            yr = _dot(ret, wr_refs[t][...])
            ys = _dot(sgo, ws_refs[t][...])
            o_ref[rows, cols] = (gr_ref[rows, cols] * yr.astype(BF16)
                                 + gs_ref[rows, cols] * ys.astype(BF16))


def _branches(retn, swish, uvs, gates, ln_g, ln_b, sgw, sgb_tab, wr, ws, tm):
    m, w = retn.shape
    d = wr.shape[1]
    row = lambda i: (i, 0)
    row1 = lambda i: (i, 1)
    const2 = lambda i: (0, 0)
    wr_specs, wr_args = _weight_slabs(wr, 0, d)
    ws_specs, ws_args = _weight_slabs(ws, 0, d)
    return pl.pallas_call(
        functools.partial(_branch_kernel, tm=tm),
        grid=(m // tm,),
        in_specs=[
            pl.BlockSpec((tm, w), row),
            pl.BlockSpec((tm, w), row),
            pl.BlockSpec((tm, w), row),
            pl.BlockSpec((tm, w), row1),
            pl.BlockSpec((tm, d), row),
            pl.BlockSpec((tm, d), row1),
            _resident((1, w), const2),
            _resident((1, w), const2),
            _resident(sgw.shape, lambda i: (0, 0, 0)),
            _resident(sgb_tab.shape, const2),
        ] + wr_specs + ws_specs,
        out_specs=pl.BlockSpec((tm, d), row),
        out_shape=jax.ShapeDtypeStruct((m, d), BF16),
        scratch_shapes=[pltpu.VMEM((tm, w), BF16)],
        compiler_params=_params("arbitrary"),
        name="branches",
    )(retn, swish, uvs, uvs, gates, gates, ln_g.reshape(1, w), ln_b.reshape(1, w), sgw, sgb_tab,
      *wr_args, *ws_args)


def _outproj_kernel(m_ref, x_ref, gt1_ref, g2_ref, sh2_ref, sc2_ref, *rest, tm):
    x1_ref, h2_ref = rest[-2:]
    wo_refs = rest[:-2]
    gain = g2_ref[...] * (1.0 + sc2_ref[0])
    for r0 in range(0, tm, SUB_ROWS):
        rows = slice(r0, r0 + SUB_ROWS)
        mr = m_ref[rows, :]
        for t, wo_ref in enumerate(wo_refs):
            cols = slice(t * WEIGHT_COLS, (t + 1) * WEIGHT_COLS)
            x1_ref[rows, cols] = x_ref[rows, cols] + gt1_ref[0, :, cols] * _dot(mr, wo_ref[...])
        h2_ref[rows, :] = _modulated_norm(x1_ref[rows, :], gain, sh2_ref[0])


def _outproj(merged, wo, x2d, mod3, mod_blks, g2, seq_len, tm):
    m, d = x2d.shape
    per = seq_len // tm
    row = lambda i: (i, 0)
    modspec = lambda blk: pl.BlockSpec((1, 1, d), lambda i: (i // per, 0, blk))
    wo_specs, wo_args = _weight_slabs(wo, 0, d)
    return pl.pallas_call(
        functools.partial(_outproj_kernel, tm=tm),
        grid=(m // tm,),
        in_specs=[
            pl.BlockSpec((tm, d), row),
            pl.BlockSpec((tm, d), row),
            modspec(mod_blks[0]), pl.BlockSpec((1, d), lambda i: (0, 0)),
            modspec(mod_blks[1]), modspec(mod_blks[2]),
        ] + wo_specs,
        out_specs=[pl.BlockSpec((tm, d), row), pl.BlockSpec((tm, d), row)],
        out_shape=[jax.ShapeDtypeStruct((m, d), F32), jax.ShapeDtypeStruct((m, d), BF16)],
        compiler_params=_params("arbitrary"),
        name="outproj",
    )(merged, x2d, mod3, g2.reshape(1, d), mod3, mod3, *wo_args)


def _ffn_up_kernel(h_ref, wa_ref, wb_ref, o_ref, *, tm):
    for r0 in range(0, tm, SUB_ROWS):
        rows = slice(r0, r0 + SUB_ROWS)
        hr = h_ref[rows, :]
        a = _dot(hr, wa_ref[...])
        b = _dot(hr, wb_ref[...])
        o_ref[rows, :] = (_silu(a) * b).astype(BF16)


def _ffn_up(h2, w_in, tm, tn):
    m, d = h2.shape
    hidden = w_in.shape[1] // 2
    nt = hidden // tn
    return pl.pallas_call(
        functools.partial(_ffn_up_kernel, tm=tm),
        grid=(m // tm, nt),
        in_specs=[
            pl.BlockSpec((tm, d), lambda i, j: (i, 0)),
            pl.BlockSpec((d, tn), lambda i, j: (0, j)),
            pl.BlockSpec((d, tn), lambda i, j: (0, nt + j)),
        ],
        out_specs=pl.BlockSpec((tm, tn), lambda i, j: (i, j)),
        out_shape=jax.ShapeDtypeStruct((m, hidden), BF16),
        compiler_params=_params("arbitrary", "arbitrary"),
        name="ffn_up",
    )(h2, w_in, w_in)


def _ffn_down_kernel(a_ref, x1_ref, gt2_ref, gf_ref, *rest, tm):
    o_ref = rest[-1]
    w_refs = rest[:-1]
    for r0 in range(0, tm, SUB_ROWS):
        rows = slice(r0, r0 + SUB_ROWS)
        ar = a_ref[rows, :]
        for t, w_ref in enumerate(w_refs):
            cols = slice(t * WEIGHT_COLS, (t + 1) * WEIGHT_COLS)
            o_ref[rows, cols] = x1_ref[rows, cols] + gt2_ref[0, :, cols] * _dot(ar, w_ref[...])
        o_ref[rows, :] = _rms(o_ref[rows, :]) * gf_ref[...]


def _ffn_down(act, w_out, x1, mod3, mod_blk, gf, seq_len, tm):
    m, d = x1.shape
    hidden = act.shape[1]
    per = seq_len // tm
    row = lambda i: (i, 0)
    w_specs, w_args = _weight_slabs(w_out, 0, d)
    return pl.pallas_call(
        functools.partial(_ffn_down_kernel, tm=tm),
        grid=(m // tm,),
        in_specs=[
            pl.BlockSpec((tm, hidden), row),
            pl.BlockSpec((tm, d), row),
            pl.BlockSpec((1, 1, d), lambda i: (i // per, 0, mod_blk)),
            pl.BlockSpec((1, d), lambda i: (0, 0)),
        ] + w_specs,
        out_specs=pl.BlockSpec((tm, d), row),
        out_shape=jax.ShapeDtypeStruct((m, d), F32),
        compiler_params=_params("arbitrary"),
        name="ffn_down",
    )(act, x1, mod3, gf.reshape(1, d), *w_args)


def _rope_tables(seq_len, dk):
    freqs = dk // 4
    rows = seq_len // GRID_W
    row = np.repeat(np.arange(rows), GRID_W)
    col = np.tile(np.arange(GRID_W), rows)
    freq = ROPE_BASE ** (-np.arange(freqs, dtype=np.float64) / freqs)
    ang = np.stack([row, col], axis=-1).astype(np.float64)[:, :, None] * freq
    cos, sin = np.cos(ang), np.sin(ang)
    cos_t = np.concatenate([cos[:, 0], cos[:, 0], cos[:, 1], cos[:, 1]], axis=-1)
    sin_t = np.concatenate([-sin[:, 0], sin[:, 0], -sin[:, 1], sin[:, 1]], axis=-1)
    return jnp.asarray(cos_t, F32), jnp.asarray(sin_t, F32)


def kernel(x, c, ctx, c_ctx, w_mod, b_mod, norm1_g, w_in, ret_decay_fwd, ret_decay_bwd,
           sg_ln_g, sg_ln_b, sg_w, sg_b, w_ret_o, w_sg_o, w_out, norm2_g, w_ffn_in, w_ffn_out,
           final_norm_g):
    batch, seq_len, d = x.shape
    ctx_len = ctx.shape[1]
    depth = w_mod.shape[0]
    assert depth == 1
    width = w_ret_o.shape[1]
    dk = width // RET_HEADS
    q_off, k_off, v_off, g_off, u_off, gr_off = (i * width for i in (0, 1, 2, 3, 4, 6))

    x2d = x.reshape(batch * seq_len, d)
    ctx2d = ctx.reshape(batch * ctx_len, d)

    pad = (-(batch + 1)) % 8
    cond = jnp.concatenate([c, c_ctx[None], jnp.zeros((pad, d), F32)], axis=0)
    b_mod2d = b_mod[0].reshape(1, -1)
    mod_a = _adaln(cond, w_mod[0], b_mod2d, 2 * d)
    mod_a3 = mod_a.reshape(mod_a.shape[0], 1, mod_a.shape[1])

    lg = jnp.stack([-jax.nn.softplus(-ret_decay_fwd[0].astype(F32)),
                    -jax.nn.softplus(-ret_decay_bwd[0].astype(F32))])

    hc = _prenorm(ctx2d, norm1_g[0], mod_a3, ctx2d.shape[0], lambda b: batch, 0, 1,
                  tl=ROW_TILE["ctx_norm"])
    w_qkv_b, kvc = _castproj(hc, w_in[0], g_off, k_off, 2 * width, width, dk ** -0.5)

    rope = _rope_tables(seq_len, dk)
    qk, h, w_g_b, w_uvs_b, w_gates_b = _proj(
        x2d, w_qkv_b, q_off, 2 * width, "rope", tm=ROW_TILE["proj_qk"], rope=rope, seq_len=seq_len,
        col_scale=(width, 2 * width, dk ** -0.5), norm=(norm1_g[0], mod_a3, 0, 1),
        sides=((w_in[0], g_off, width), (w_in[0], u_off, 2 * width), (w_in[0], gr_off, 2 * d)),
        name="proj_qk")
    v, wr_b, ws_b, wo_b = _proj(h, w_qkv_b, v_off, width, "none", tm=ROW_TILE["proj_v"],
                                sides=(w_ret_o[0], w_sg_o[0], w_out[0]), name="proj_v")
    g, w_down_b = _proj(h, w_g_b, 0, width, "silu", tm=ROW_TILE["proj_g"],
                        sides=(w_ffn_out[0],), name="proj_g")
    uvs, mod_b = _proj(h, w_uvs_b, 0, 2 * width, "gelu", tm=ROW_TILE["proj_uvs"],
                       mod_job=(cond, w_mod[0], b_mod2d, 2 * d), name="proj_uvs")
    mod_b3 = mod_b.reshape(mod_b.shape[0], 1, mod_b.shape[1])
    gates, w_up_b = _proj(h, w_gates_b, 0, 2 * d, "sigmoid", tm=ROW_TILE["proj_gates"],
                          sides=(w_ffn_in[0],), name="proj_gates")

    retn = _retention(lg, qk, v, kvc, batch, seq_len, ctx_len)

    sgb_tab = jnp.repeat(sg_b[0].T, width // SG_GROUPS, axis=1)
    merged = _branches(retn, g, uvs, gates, sg_ln_g[0], sg_ln_b[0], sg_w[0], sgb_tab, wr_b, ws_b,
                       tm=ROW_TILE["branches"])
    x1, h2 = _outproj(merged, wo_b, x2d, mod_b3, (0, 1, 2), norm2_g[0], seq_len,
                      tm=ROW_TILE["outproj"])

    act = _ffn_up(h2, w_up_b, tm=ROW_TILE["ffn_up"], tn=FFN_UP_COLS)
    out = _ffn_down(act, w_down_b, x1, mod_b3, 3, final_norm_g, seq_len,
                    tm=ROW_TILE["ffn_down"])
    return out.reshape(batch, seq_len, d)
```

```python
import functools
import math

import jax
import jax.numpy as jnp
import numpy as np
from jax import lax
from jax.experimental import pallas as pl
from jax.experimental.pallas import tpu as pltpu

F32 = jnp.float32
BF16 = jnp.bfloat16

EPS = 1e-6
GRID_W = 64
ROPE_BASE = 10000.0
RET_HEADS = 8
SG_GROUPS = 8
SG_CHUNK = 128
V7X_VMEM_BYTES = 64 * 1024 * 1024
VMEM_LIMIT = V7X_VMEM_BYTES - 8 * 1024 * 1024
MXU_N = 256
RET_C = MXU_N
SUB_ROWS = 256
WEIGHT_COLS = 512
ROW_TILE = dict(ctx_norm=1024, proj_qk=512, proj_v=1024, proj_g=1024, proj_uvs=1024,
                proj_gates=512, branches=512, outproj=512, ffn_up=4096, ffn_down=512)
FFN_UP_COLS = 512
ROW_GROUP = dict(proj_v=512, branches=128, outproj=512, ffn_up=128)
RET_HEADS_PER_STEP = 2


def _params(*sem):
    return pltpu.CompilerParams(dimension_semantics=sem, vmem_limit_bytes=VMEM_LIMIT)


def _sigmoid(x):
    return 1.0 / (1.0 + jnp.exp(-x))


def _silu(x):
    return x * _sigmoid(x)


def _gelu_tanh(x):
    b = -2.0 * math.sqrt(2.0 / math.pi) * math.log2(math.e)
    a = b * 0.044715
    return x / (1.0 + jnp.exp2(x * (a * (x * x) + b)))


def _rms(x):
    return x * lax.rsqrt(jnp.mean(x * x, axis=-1, keepdims=True) + EPS)


def _dot(a, b):
    return jnp.dot(a, b, preferred_element_type=F32)


def _resident(shape, index_map):
    return pl.BlockSpec(shape, index_map, pipeline_mode=pl.Buffered(1))


def _weight_slabs(w, col0, ncols):
    assert col0 % WEIGHT_COLS == 0 and ncols % WEIGHT_COLS == 0
    n = ncols // WEIGHT_COLS
    specs = [_resident((w.shape[0], WEIGHT_COLS), lambda i, t=t: (0, col0 // WEIGHT_COLS + t))
             for t in range(n)]
    return specs, [w] * n


def _adaln_kernel(c_ref, w_ref, b_ref, o_ref):
    s = _silu(c_ref[...]).astype(BF16)
    o_ref[...] = _dot(s, w_ref[...].astype(BF16)) + b_ref[...]


def _adaln(cond, w_mod, b_mod2d, ncols, tn=1024):
    rows, d = cond.shape
    return pl.pallas_call(
        _adaln_kernel,
        grid=(ncols // tn,),
        in_specs=[
            pl.BlockSpec((rows, d), lambda j: (0, 0)),
            pl.BlockSpec((d, tn), lambda j: (0, j)),
            pl.BlockSpec((1, tn), lambda j: (0, j)),
        ],
        out_specs=pl.BlockSpec((rows, tn), lambda j: (0, j)),
        out_shape=jax.ShapeDtypeStruct((rows, ncols), F32),
        compiler_params=_params("arbitrary"),
        name="adaln",
    )(cond, w_mod, b_mod2d)


def _modulated_norm(x, gain, sh):
    return (_rms(x) * gain + sh).astype(BF16)


def _prenorm_kernel(x_ref, g_ref, sh_ref, sc_ref, o_ref):
    o_ref[...] = _modulated_norm(x_ref[...], g_ref[...] * (1.0 + sc_ref[0]), sh_ref[0])


def _prenorm(x2d, g, mod3, rows_per_batch, mod_row, shift_blk, scale_blk, tl):
    m, d = x2d.shape
    per = rows_per_batch // tl
    return pl.pallas_call(
        _prenorm_kernel,
        grid=(m // tl,),
        in_specs=[
            pl.BlockSpec((tl, d), lambda i: (i, 0)),
            pl.BlockSpec((1, d), lambda i: (0, 0)),
            pl.BlockSpec((1, 1, d), lambda i: (mod_row(i // per), 0, shift_blk)),
            pl.BlockSpec((1, 1, d), lambda i: (mod_row(i // per), 0, scale_blk)),
        ],
        out_specs=pl.BlockSpec((tl, d), lambda i: (i, 0)),
        out_shape=jax.ShapeDtypeStruct((m, d), BF16),
        compiler_params=_params("arbitrary"),
        name="prenorm",
    )(x2d, g.reshape(1, d), mod3, mod3)


def _activate(acc, act):
    if act == "silu":
        return _silu(acc)
    if act == "gelu":
        return _gelu_tanh(acc)
    if act == "sigmoid":
        return _sigmoid(acc)
    return acc


def _proj_kernel(*refs, act, col_scale, tm, sub, ncols, fused_norm, n_side, mod_job):
    refs = list(refs)
    if mod_job:
        mo_ref = refs.pop()
    side_out = [refs.pop() for _ in range(n_side)][::-1]
    if fused_norm:
        x_ref, g_ref, sh_ref, sc_ref = refs[:4]
        refs = refs[4:]
        h_ref = refs.pop()
    else:
        h_ref = refs.pop(0)
    w_refs = [refs.pop(0) for _ in range(ncols // WEIGHT_COLS)]
    o_ref = refs.pop()
    if mod_job:
        c_ref, wm_ref, bm_ref = refs[-3:]
        refs = refs[:-3]
        _adaln_kernel(c_ref, wm_ref, bm_ref, mo_ref)
    side_in = [refs.pop() for _ in range(n_side)][::-1]
    if act == "rope":
        cos_ref, sin_ref = refs
    for src, dst in zip(side_in, side_out):
        dst[...] = src[...].astype(BF16)
    half = MXU_N // 2
    if fused_norm:
        gain = g_ref[...] * (1.0 + sc_ref[0])
    for r0 in range(0, tm, sub):
        rows = slice(r0, r0 + sub)
        if fused_norm:
            h_ref[rows, :] = _modulated_norm(x_ref[rows, :], gain, sh_ref[0])
        hr = h_ref[rows, :]
        for n0 in range(0, ncols, WEIGHT_COLS):
            acc = _dot(hr, w_refs[n0 // WEIGHT_COLS][...])
            if col_scale is not None and col_scale[0] <= n0 < col_scale[1]:
                acc = acc * col_scale[2]
            acc = _activate(acc, act)
            if act == "rope":
                for s in range(WEIGHT_COLS // half):
                    lanes = slice((s % 2) * half, (s % 2 + 1) * half)
                    xa = acc[:, s * half:(s + 1) * half]
                    ya = (xa * cos_ref[rows, lanes]
                          + pltpu.roll(xa, half // 2, axis=1) * sin_ref[rows, lanes])
                    o_ref[rows, n0 + s * half:n0 + (s + 1) * half] = ya.astype(BF16)
            else:
                o_ref[rows, n0:n0 + WEIGHT_COLS] = acc.astype(BF16)


def _proj(h, w, col0, ncols, act, *, tm, rope=None, seq_len=None, col_scale=None,
          norm=None, sides=(), mod_job=None, name="proj"):
    m, k = h.shape
    steps = m // tm
    sub = ROW_GROUP.get(name, SUB_ROWS)
    in_specs, args = [], []
    if norm is not None:
        g, mod3, shift_blk, scale_blk = norm
        per = seq_len // tm
        in_specs += [
            pl.BlockSpec((tm, k), lambda i: (i, 0)),
            pl.BlockSpec((1, k), lambda i: (0, 0)),
            pl.BlockSpec((1, 1, k), lambda i: (i // per, 0, shift_blk)),
            pl.BlockSpec((1, 1, k), lambda i: (i // per, 0, scale_blk)),
        ]
        args += [h, g.reshape(1, k), mod3, mod3]
    else:
        in_specs.append(pl.BlockSpec((tm, k), lambda i: (i, 0)))
        args.append(h)
    w_specs, w_args = _weight_slabs(w, col0, ncols)
    in_specs += w_specs
    args += w_args
    if act == "rope":
        per_l = seq_len // tm
        spec = pl.BlockSpec((tm, MXU_N), lambda i: (i % per_l, 0))
        in_specs += [spec, spec]
        args += list(rope)
    out_specs = [pl.BlockSpec((tm, ncols), lambda i: (i, 0))]
    out_shape = [jax.ShapeDtypeStruct((m, ncols), BF16)]
    if norm is not None:
        out_specs.append(pl.BlockSpec((tm, k), lambda i: (i, 0)))
        out_shape.append(jax.ShapeDtypeStruct((m, k), BF16))
    for side in sides:
        arr, c0, c = side if isinstance(side, tuple) else (side, 0, side.shape[1])
        r = arr.shape[0]
        assert r % (steps * 16) == 0 and c0 % c == 0
        in_specs.append(pl.BlockSpec((r // steps, c), lambda i, cb=c0 // c: (i, cb)))
        args.append(arr)
        out_specs.append(pl.BlockSpec((r // steps, c), lambda i: (i, 0)))
        out_shape.append(jax.ShapeDtypeStruct((r, c), BF16))
    if mod_job is not None:
        cond, w_mod, b_mod2d, c0 = mod_job
        mc = (w_mod.shape[1] - c0) // steps
        assert mc * steps == w_mod.shape[1] - c0 and c0 % mc == 0
        in_specs += [
            pl.BlockSpec(cond.shape, lambda i: (0, 0)),
            pl.BlockSpec((w_mod.shape[0], mc), lambda i: (0, c0 // mc + i)),
            pl.BlockSpec((1, mc), lambda i: (0, c0 // mc + i)),
        ]
        args += [cond, w_mod, b_mod2d]
        out_specs.append(pl.BlockSpec((cond.shape[0], mc), lambda i: (0, i)))
        out_shape.append(jax.ShapeDtypeStruct((cond.shape[0], mc * steps), F32))
    outs = pl.pallas_call(
        functools.partial(_proj_kernel, act=act, col_scale=col_scale, tm=tm, sub=sub, ncols=ncols,
                          fused_norm=norm is not None, n_side=len(sides),
                          mod_job=mod_job is not None),
        grid=(steps,),
        in_specs=in_specs,
        out_specs=out_specs,
        out_shape=out_shape,
        compiler_params=_params("arbitrary"),
        name=name,
    )(*args)
    return outs[0] if len(outs) == 1 else outs


def _castproj_kernel(hc_ref, w_ref, wb_ref, kv_ref, *, m, lo, hi, k_blocks, scale):
    j = pl.program_id(0)
    wb_ref[...] = w_ref[...].astype(BF16)

    @pl.when(jnp.logical_and(j >= lo, j < hi))
    def _():
        mult = jnp.where(j < lo + k_blocks, scale, 1.0).astype(F32)
        for r0 in range(0, m, SUB_ROWS):
            rows = slice(r0, r0 + SUB_ROWS)
            kv_ref[rows, :] = (_dot(hc_ref[rows, :], wb_ref[...]) * mult).astype(BF16)


def _castproj(hc, w_f32, n, k_col0, kv_cols, k_cols, scale, tn=1024):
    m, k = hc.shape
    lo, hi = k_col0 // tn, (k_col0 + kv_cols) // tn
    assert hi * tn <= n
    return pl.pallas_call(
        functools.partial(_castproj_kernel, m=m, lo=lo, hi=hi, k_blocks=k_cols // tn, scale=scale),
        grid=(n // tn,),
        in_specs=[
            _resident((m, k), lambda j: (0, 0)),
            pl.BlockSpec((k, tn), lambda j: (0, j)),
        ],
        out_specs=[
            pl.BlockSpec((k, tn), lambda j: (0, j)),
            pl.BlockSpec((m, tn), lambda j: (0, jnp.clip(j - lo, 0, hi - lo - 1))),
        ],
        out_shape=[jax.ShapeDtypeStruct((k, n), BF16), jax.ShapeDtypeStruct((m, kv_cols), BF16)],
        compiler_params=_params("arbitrary"),
        name="castproj_ctx",
    )(hc, w_f32)


def _dot_t(a, b):
    return lax.dot_general(a, b, (((0,), (0,)), ((), ())), preferred_element_type=F32)


def _dot_nt(a, b):
    return lax.dot_general(a, b, (((1,), (1,)), ((), ())), preferred_element_type=F32)


def _ret_kernel(lg_ref, q_ref, k_ref, v_ref, kc_ref, vc_ref, o_ref, sf_ref, sb_ref,
                *, n_chunks, heads_per_step):
    c = RET_C
    ri = lax.broadcasted_iota(jnp.int32, (c, c), 0).astype(F32)
    ci = lax.broadcasted_iota(jnp.int32, (c, c), 1).astype(F32)
    diff = ri - ci
    zero = jnp.zeros((1, c), F32)

    def kv(kn, vn, dec):
        return _dot_t(kn * dec, vn)

    for t in range(heads_per_step):
        hd = pl.program_id(1) * heads_per_step + t
        hc = slice(t * c, (t + 1) * c)
        lgf = lg_ref[0, hd]
        lgb = lg_ref[1, hd]
        dmask = jnp.where(diff >= 0.0,
                          jnp.exp(lgf * jnp.maximum(diff, 0.0)),
                          jnp.exp(lgb * jnp.maximum(-diff, 0.0))).astype(BF16)
        qdf = jnp.exp(lgf * (ri + 1.0)).astype(BF16)
        qdb = jnp.exp(lgb * (c - ri)).astype(BF16)
        kdf = jnp.exp(lgf * (c - 1.0 - ri)).astype(BF16)
        kdb = jnp.exp(lgb * ri).astype(BF16)
        cdf = jnp.exp(zero + lgf * c)
        cdb = jnp.exp(zero + lgb * c)

        kc = kc_ref[:, hc]
        vc = vc_ref[:, hc]
        sf_ref[t] = kv(kc, vc, kdf)
        sb = kv(kc, vc, kdb)
        for n in reversed(range(n_chunks)):
            rows = slice(n * c, (n + 1) * c)
            sb_ref[t, n] = sb.astype(BF16)
            if n > 0:
                sb = sb * cdb + kv(k_ref[rows, hc], v_ref[rows, hc], kdb)
        for n in range(n_chunks):
            rows = slice(n * c, (n + 1) * c)
            qn = q_ref[rows, hc]
            kn = k_ref[rows, hc]
            vn = v_ref[rows, hc]
            a = _dot_nt(qn, kn).astype(BF16) * dmask
            o = (_dot(a, vn)
                 + _dot(qn * qdf, sf_ref[t].astype(BF16))
                 + _dot(qn * qdb, sb_ref[t, n]))
            o_ref[rows, hc] = _rms(o).astype(BF16)
            if n + 1 < n_chunks:
                sf_ref[t] = sf_ref[t] * cdf + kv(kn, vn, kdf)


def _retention(lg, qk, v, kvc, batch, seq_len, ctx_len, heads_per_step=RET_HEADS_PER_STEP):
    m = qk.shape[0]
    hd = RET_HEADS
    dk = qk.shape[1] // (2 * hd)
    dv = v.shape[1] // hd
    assert dk == RET_C and dv == RET_C and ctx_len == RET_C and seq_len % RET_C == 0
    n_chunks = seq_len // RET_C
    hps = heads_per_step
    hb = hd // hps
    return pl.pallas_call(
        functools.partial(_ret_kernel, n_chunks=n_chunks, heads_per_step=hps),
        grid=(batch, hb),
        in_specs=[
            pl.BlockSpec(memory_space=pltpu.SMEM),
            pl.BlockSpec((seq_len, hps * dk), lambda b, h: (b, h)),
            pl.BlockSpec((seq_len, hps * dk), lambda b, h: (b, hb + h)),
            pl.BlockSpec((seq_len, hps * dv), lambda b, h: (b, h)),
            pl.BlockSpec((ctx_len, hps * dk), lambda b, h: (b, h)),
            pl.BlockSpec((ctx_len, hps * dv), lambda b, h: (b, hb + h)),
        ],
        out_specs=pl.BlockSpec((seq_len, hps * dv), lambda b, h: (b, h)),
        out_shape=jax.ShapeDtypeStruct((m, hd * dv), BF16),
        scratch_shapes=[
            pltpu.VMEM((hps, dk, dv), F32),
            pltpu.VMEM((hps, n_chunks, dk, dv), BF16),
        ],
        compiler_params=_params("arbitrary", "arbitrary"),
        name="retention",
    )(lg, qk, qk, v, kvc, kvc)


def _branch_kernel(ret_ref, sw_ref, u_ref, vs_ref, gr_ref, gs_ref, lng_ref, lnb_ref, sgw_ref,
                   sgb_ref, *rest, tm, sub):
    o_ref, sgo_ref = rest[-2:]
    n_slabs = (len(rest) - 2) // 2
    wr_refs, ws_refs = rest[:n_slabs], rest[n_slabs:2 * n_slabs]
    width = vs_ref.shape[1]
    gd = width // SG_GROUPS
    for r0 in range(0, tm, sub):
        rows = slice(r0, r0 + sub)
        vs = vs_ref[rows, :].astype(F32)
        mu = jnp.mean(vs, axis=-1, keepdims=True)
        cen = vs - mu
        var = jnp.mean(cen * cen, axis=-1, keepdims=True)
        vn = ((cen * lax.rsqrt(var + EPS)) * lng_ref[...] + lnb_ref[...]).astype(BF16)
        for c0 in range(0, sub, SG_CHUNK):
            crow = slice(r0 + c0, r0 + c0 + SG_CHUNK)
            for gi in range(SG_GROUPS):
                cols = slice(gi * gd, (gi + 1) * gd)
                mixed = _dot(sgw_ref[gi].astype(BF16), vn[c0:c0 + SG_CHUNK, cols]) + sgb_ref[:, cols]
                sgo_ref[crow, cols] = u_ref[crow, cols] * mixed.astype(BF16)
        ret = ret_ref[rows, :] * sw_ref[rows, :]
        sgo = sgo_ref[rows, :]
        for t in range(n_slabs):
            cols = slice(t * WEIGHT_COLS, (t + 1) * WEIGHT_COLS)
            yr = _dot(ret, wr_refs[t][...])
            ys = _dot(sgo, ws_refs[t][...])
            o_ref[rows, cols] = (gr_ref[rows, cols] * yr.astype(BF16)
                                 + gs_ref[rows, cols] * ys.astype(BF16))


def _branches(retn, swish, uvs, gates, ln_g, ln_b, sgw, sgb_tab, wr, ws, tm):
    m, w = retn.shape
    d = wr.shape[1]
    row = lambda i: (i, 0)
    row1 = lambda i: (i, 1)
    const2 = lambda i: (0, 0)
    wr_specs, wr_args = _weight_slabs(wr, 0, d)
    ws_specs, ws_args = _weight_slabs(ws, 0, d)
    return pl.pallas_call(
        functools.partial(_branch_kernel, tm=tm, sub=ROW_GROUP.get("branches", SUB_ROWS)),
        grid=(m // tm,),
        in_specs=[
            pl.BlockSpec((tm, w), row),
            pl.BlockSpec((tm, w), row),
            pl.BlockSpec((tm, w), row),
            pl.BlockSpec((tm, w), row1),
            pl.BlockSpec((tm, d), row),
            pl.BlockSpec((tm, d), row1),
            _resident((1, w), const2),
            _resident((1, w), const2),
            _resident(sgw.shape, lambda i: (0, 0, 0)),
            _resident(sgb_tab.shape, const2),
        ] + wr_specs + ws_specs,
        out_specs=pl.BlockSpec((tm, d), row),
        out_shape=jax.ShapeDtypeStruct((m, d), BF16),
        scratch_shapes=[pltpu.VMEM((tm, w), BF16)],
        compiler_params=_params("arbitrary"),
        name="branches",
    )(retn, swish, uvs, uvs, gates, gates, ln_g.reshape(1, w), ln_b.reshape(1, w), sgw, sgb_tab,
      *wr_args, *ws_args)


def _outproj_kernel(m_ref, x_ref, gt1_ref, g2_ref, sh2_ref, sc2_ref, *rest, tm, sub):
    x1_ref, h2_ref = rest[-2:]
    wo_refs = rest[:-2]
    gain = g2_ref[...] * (1.0 + sc2_ref[0])
    for r0 in range(0, tm, sub):
        rows = slice(r0, r0 + sub)
        mr = m_ref[rows, :]
        for t, wo_ref in enumerate(wo_refs):
            cols = slice(t * WEIGHT_COLS, (t + 1) * WEIGHT_COLS)
            x1_ref[rows, cols] = x_ref[rows, cols] + gt1_ref[0, :, cols] * _dot(mr, wo_ref[...])
        h2_ref[rows, :] = _modulated_norm(x1_ref[rows, :], gain, sh2_ref[0])


def _outproj(merged, wo, x2d, mod3, mod_blks, g2, seq_len, tm):
    m, d = x2d.shape
    per = seq_len // tm
    row = lambda i: (i, 0)
    modspec = lambda blk: pl.BlockSpec((1, 1, d), lambda i: (i // per, 0, blk))
    wo_specs, wo_args = _weight_slabs(wo, 0, d)
    return pl.pallas_call(
        functools.partial(_outproj_kernel, tm=tm, sub=ROW_GROUP.get("outproj", SUB_ROWS)),
        grid=(m // tm,),
        in_specs=[
            pl.BlockSpec((tm, d), row),
            pl.BlockSpec((tm, d), row),
            modspec(mod_blks[0]), pl.BlockSpec((1, d), lambda i: (0, 0)),
            modspec(mod_blks[1]), modspec(mod_blks[2]),
        ] + wo_specs,
        out_specs=[pl.BlockSpec((tm, d), row), pl.BlockSpec((tm, d), row)],
        out_shape=[jax.ShapeDtypeStruct((m, d), F32), jax.ShapeDtypeStruct((m, d), BF16)],
        compiler_params=_params("arbitrary"),
        name="outproj",
    )(merged, x2d, mod3, g2.reshape(1, d), mod3, mod3, *wo_args)


def _ffn_up_kernel(h_ref, wa_ref, wb_ref, o_ref, *, tm, sub):
    for r0 in range(0, tm, sub):
        rows = slice(r0, r0 + sub)
        hr = h_ref[rows, :]
        a = _dot(hr, wa_ref[...])
        b = _dot(hr, wb_ref[...])
        o_ref[rows, :] = (_silu(a) * b).astype(BF16)


def _ffn_up(h2, w_in, tm, tn):
    m, d = h2.shape
    hidden = w_in.shape[1] // 2
    nt = hidden // tn
    return pl.pallas_call(
        functools.partial(_ffn_up_kernel, tm=tm, sub=ROW_GROUP.get("ffn_up", SUB_ROWS)),
        grid=(m // tm, nt),
        in_specs=[
            pl.BlockSpec((tm, d), lambda i, j: (i, 0)),
            pl.BlockSpec((d, tn), lambda i, j: (0, j)),
            pl.BlockSpec((d, tn), lambda i, j: (0, nt + j)),
        ],
        out_specs=pl.BlockSpec((tm, tn), lambda i, j: (i, j)),
        out_shape=jax.ShapeDtypeStruct((m, hidden), BF16),
        compiler_params=_params("arbitrary", "arbitrary"),
        name="ffn_up",
    )(h2, w_in, w_in)


def _ffn_down_kernel(a_ref, x1_ref, gt2_ref, gf_ref, *rest, tm, sub):
    o_ref = rest[-1]
    w_refs = rest[:-1]
    for r0 in range(0, tm, sub):
        rows = slice(r0, r0 + sub)
        ar = a_ref[rows, :]
        for t, w_ref in enumerate(w_refs):
            cols = slice(t * WEIGHT_COLS, (t + 1) * WEIGHT_COLS)
            o_ref[rows, cols] = x1_ref[rows, cols] + gt2_ref[0, :, cols] * _dot(ar, w_ref[...])
        o_ref[rows, :] = _rms(o_ref[rows, :]) * gf_ref[...]


def _ffn_down(act, w_out, x1, mod3, mod_blk, gf, seq_len, tm):
    m, d = x1.shape
    hidden = act.shape[1]
    per = seq_len // tm
    row = lambda i: (i, 0)
    w_specs, w_args = _weight_slabs(w_out, 0, d)
    return pl.pallas_call(
        functools.partial(_ffn_down_kernel, tm=tm, sub=ROW_GROUP.get("ffn_down", SUB_ROWS)),
        grid=(m // tm,),
        in_specs=[
            pl.BlockSpec((tm, hidden), row),
            pl.BlockSpec((tm, d), row),
            pl.BlockSpec((1, 1, d), lambda i: (i // per, 0, mod_blk)),
            pl.BlockSpec((1, d), lambda i: (0, 0)),
        ] + w_specs,
        out_specs=pl.BlockSpec((tm, d), row),
        out_shape=jax.ShapeDtypeStruct((m, d), F32),
        compiler_params=_params("arbitrary"),
        name="ffn_down",
    )(act, x1, mod3, gf.reshape(1, d), *w_args)


def _rope_tables(seq_len, dk):
    freqs = dk // 4
    rows = seq_len // GRID_W
    row = np.repeat(np.arange(rows), GRID_W)
    col = np.tile(np.arange(GRID_W), rows)
    freq = ROPE_BASE ** (-np.arange(freqs, dtype=np.float64) / freqs)
    ang = np.stack([row, col], axis=-1).astype(np.float64)[:, :, None] * freq
    cos, sin = np.cos(ang), np.sin(ang)
    cos_t = np.concatenate([cos[:, 0], cos[:, 0], cos[:, 1], cos[:, 1]], axis=-1)
    sin_t = np.concatenate([-sin[:, 0], sin[:, 0], -sin[:, 1], sin[:, 1]], axis=-1)
    return jnp.asarray(cos_t, F32), jnp.asarray(sin_t, F32)


def kernel(x, c, ctx, c_ctx, w_mod, b_mod, norm1_g, w_in, ret_decay_fwd, ret_decay_bwd,
           sg_ln_g, sg_ln_b, sg_w, sg_b, w_ret_o, w_sg_o, w_out, norm2_g, w_ffn_in, w_ffn_out,
           final_norm_g):
    batch, seq_len, d = x.shape
    ctx_len = ctx.shape[1]
    depth = w_mod.shape[0]
    assert depth == 1
    width = w_ret_o.shape[1]
    dk = width // RET_HEADS
    q_off, k_off, v_off, g_off, u_off, gr_off = (i * width for i in (0, 1, 2, 3, 4, 6))

    x2d = x.reshape(batch * seq_len, d)
    ctx2d = ctx.reshape(batch * ctx_len, d)

    pad = (-(batch + 1)) % 8
    cond = jnp.concatenate([c, c_ctx[None], jnp.zeros((pad, d), F32)], axis=0)
    b_mod2d = b_mod[0].reshape(1, -1)
    mod_a = _adaln(cond, w_mod[0], b_mod2d, 2 * d)
    mod_a3 = mod_a.reshape(mod_a.shape[0], 1, mod_a.shape[1])

    lg = jnp.stack([-jax.nn.softplus(-ret_decay_fwd[0].astype(F32)),
                    -jax.nn.softplus(-ret_decay_bwd[0].astype(F32))])

    hc = _prenorm(ctx2d, norm1_g[0], mod_a3, ctx2d.shape[0], lambda b: batch, 0, 1,
                  tl=ROW_TILE["ctx_norm"])
    w_qkv_b, kvc = _castproj(hc, w_in[0], g_off, k_off, 2 * width, width, dk ** -0.5)

    rope = _rope_tables(seq_len, dk)
    qk, h, w_g_b, w_uvs_b, w_gates_b = _proj(
        x2d, w_qkv_b, q_off, 2 * width, "rope", tm=ROW_TILE["proj_qk"], rope=rope, seq_len=seq_len,
        col_scale=(width, 2 * width, dk ** -0.5), norm=(norm1_g[0], mod_a3, 0, 1),
        sides=((w_in[0], g_off, width), (w_in[0], u_off, 2 * width), (w_in[0], gr_off, 2 * d)),
        name="proj_qk")
    v, wr_b, ws_b, wo_b = _proj(h, w_qkv_b, v_off, width, "none", tm=ROW_TILE["proj_v"],
                                sides=(w_ret_o[0], w_sg_o[0], w_out[0]), name="proj_v")
    g, w_down_b = _proj(h, w_g_b, 0, width, "silu", tm=ROW_TILE["proj_g"],
                        sides=(w_ffn_out[0],), name="proj_g")
    uvs, mod_b = _proj(h, w_uvs_b, 0, 2 * width, "gelu", tm=ROW_TILE["proj_uvs"],
                       mod_job=(cond, w_mod[0], b_mod2d, 2 * d), name="proj_uvs")
    mod_b3 = mod_b.reshape(mod_b.shape[0], 1, mod_b.shape[1])
    gates, w_up_b = _proj(h, w_gates_b, 0, 2 * d, "sigmoid", tm=ROW_TILE["proj_gates"],
                          sides=(w_ffn_in[0],), name="proj_gates")

    retn = _retention(lg, qk, v, kvc, batch, seq_len, ctx_len)

    sgb_tab = jnp.repeat(sg_b[0].T, width // SG_GROUPS, axis=1)
    merged = _branches(retn, g, uvs, gates, sg_ln_g[0], sg_ln_b[0], sg_w[0], sgb_tab, wr_b, ws_b,
                       tm=ROW_TILE["branches"])
    x1, h2 = _outproj(merged, wo_b, x2d, mod_b3, (0, 1, 2), norm2_g[0], seq_len,
                      tm=ROW_TILE["outproj"])

    act = _ffn_up(h2, w_up_b, tm=ROW_TILE["ffn_up"], tn=FFN_UP_COLS)
    out = _ffn_down(act, w_down_b, x1, mod_b3, 3, final_norm_g, seq_len,
                    tm=ROW_TILE["ffn_down"])
    return out.reshape(batch, seq_len, d)
```

```python
import functools
import math

import jax
import jax.numpy as jnp
import numpy as np
from jax import lax
from jax.experimental import pallas as pl
from jax.experimental.pallas import tpu as pltpu

F32 = jnp.float32
BF16 = jnp.bfloat16

EPS = 1e-6
GRID_W = 64
ROPE_BASE = 10000.0
RET_HEADS = 8
SG_GROUPS = 8
SG_CHUNK = 128
V7X_VMEM_BYTES = 64 * 1024 * 1024
VMEM_LIMIT = V7X_VMEM_BYTES - 8 * 1024 * 1024
MXU_N = 256
RET_C = MXU_N
SUB_ROWS = 256
WEIGHT_COLS = 512
ROW_TILE = dict(ctx_norm=1024, proj_qk=512, proj_v=1024, proj_g=1024, proj_uvs=1024,
                proj_gates=512, branches=512, outproj=512, ffn_up=4096, ffn_down=512)
FFN_UP_COLS = 512
ROW_GROUP = dict(proj_v=512, branches=128, outproj=512, ffn_up=128)
RET_HEADS_PER_STEP = 2


def _params(*sem):
    return pltpu.CompilerParams(dimension_semantics=sem, vmem_limit_bytes=VMEM_LIMIT)


def _sigmoid(x):
    return 1.0 / (1.0 + jnp.exp(-x))


def _silu(x):
    return x * _sigmoid(x)


def _gelu_tanh(x):
    b = -2.0 * math.sqrt(2.0 / math.pi) * math.log2(math.e)
    a = b * 0.044715
    return x / (1.0 + jnp.exp2(x * (a * (x * x) + b)))


def _rms(x):
    return x * lax.rsqrt(jnp.mean(x * x, axis=-1, keepdims=True) + EPS)


def _dot(a, b):
    return jnp.dot(a, b, preferred_element_type=F32)


def _resident(shape, index_map):
    return pl.BlockSpec(shape, index_map, pipeline_mode=pl.Buffered(1))


def _weight_slabs(w, col0, ncols):
    assert col0 % WEIGHT_COLS == 0 and ncols % WEIGHT_COLS == 0
    n = ncols // WEIGHT_COLS
    specs = [_resident((w.shape[0], WEIGHT_COLS), lambda i, t=t: (0, col0 // WEIGHT_COLS + t))
             for t in range(n)]
    return specs, [w] * n


def _adaln_kernel(c_ref, w_ref, b_ref, o_ref):
    s = _silu(c_ref[...]).astype(BF16)
    o_ref[...] = _dot(s, w_ref[...].astype(BF16)) + b_ref[...]


def _adaln(cond, w_mod, b_mod2d, ncols, tn=2048):
    rows, d = cond.shape
    return pl.pallas_call(
        _adaln_kernel,
        grid=(ncols // tn,),
        in_specs=[
            pl.BlockSpec((rows, d), lambda j: (0, 0)),
            pl.BlockSpec((d, tn), lambda j: (0, j)),
            pl.BlockSpec((1, tn), lambda j: (0, j)),
        ],
        out_specs=pl.BlockSpec((rows, tn), lambda j: (0, j)),
        out_shape=jax.ShapeDtypeStruct((rows, ncols), F32),
        compiler_params=_params("arbitrary"),
        name="adaln",
    )(cond, w_mod, b_mod2d)


def _modulated_norm(x, gain, sh):
    return (_rms(x) * gain + sh).astype(BF16)


def _prenorm_kernel(x_ref, g_ref, sh_ref, sc_ref, o_ref):
    o_ref[...] = _modulated_norm(x_ref[...], g_ref[...] * (1.0 + sc_ref[0]), sh_ref[0])


def _prenorm(x2d, g, mod3, rows_per_batch, mod_row, shift_blk, scale_blk, tl):
    m, d = x2d.shape
    per = rows_per_batch // tl
    return pl.pallas_call(
        _prenorm_kernel,
        grid=(m // tl,),
        in_specs=[
            pl.BlockSpec((tl, d), lambda i: (i, 0)),
            pl.BlockSpec((1, d), lambda i: (0, 0)),
            pl.BlockSpec((1, 1, d), lambda i: (mod_row(i // per), 0, shift_blk)),
            pl.BlockSpec((1, 1, d), lambda i: (mod_row(i // per), 0, scale_blk)),
        ],
        out_specs=pl.BlockSpec((tl, d), lambda i: (i, 0)),
        out_shape=jax.ShapeDtypeStruct((m, d), BF16),
        compiler_params=_params("arbitrary"),
        name="prenorm",
    )(x2d, g.reshape(1, d), mod3, mod3)


def _activate(acc, act):
    if act == "silu":
        return _silu(acc)
    if act == "gelu":
        return _gelu_tanh(acc)
    if act == "sigmoid":
        return _sigmoid(acc)
    return acc


def _proj_kernel(*refs, act, col_scale, tm, sub, ncols, fused_norm, n_side, mod_job):
    refs = list(refs)
    if mod_job:
        mo_ref = refs.pop()
    side_out = [refs.pop() for _ in range(n_side)][::-1]
    if fused_norm:
        x_ref, g_ref, sh_ref, sc_ref = refs[:4]
        refs = refs[4:]
        h_ref = refs.pop()
    else:
        h_ref = refs.pop(0)
    w_refs = [refs.pop(0) for _ in range(ncols // WEIGHT_COLS)]
    o_ref = refs.pop()
    if mod_job:
        c_ref, wm_ref, bm_ref = refs[-3:]
        refs = refs[:-3]
        _adaln_kernel(c_ref, wm_ref, bm_ref, mo_ref)
    side_in = [refs.pop() for _ in range(n_side)][::-1]
    if act == "rope":
        cos_ref, sin_ref = refs
    for src, dst in zip(side_in, side_out):
        dst[...] = src[...].astype(BF16)
    half = MXU_N // 2
    if fused_norm:
        gain = g_ref[...] * (1.0 + sc_ref[0])
    for r0 in range(0, tm, sub):
        rows = slice(r0, r0 + sub)
        if fused_norm:
            h_ref[rows, :] = _modulated_norm(x_ref[rows, :], gain, sh_ref[0])
        hr = h_ref[rows, :]
        for n0 in range(0, ncols, WEIGHT_COLS):
            acc = _dot(hr, w_refs[n0 // WEIGHT_COLS][...])
            if col_scale is not None and col_scale[0] <= n0 < col_scale[1]:
                acc = acc * col_scale[2]
            acc = _activate(acc, act)
            if act == "rope":
                for s in range(WEIGHT_COLS // half):
                    lanes = slice((s % 2) * half, (s % 2 + 1) * half)
                    xa = acc[:, s * half:(s + 1) * half]
                    ya = (xa * cos_ref[rows, lanes]
                          + pltpu.roll(xa, half // 2, axis=1) * sin_ref[rows, lanes])
                    o_ref[rows, n0 + s * half:n0 + (s + 1) * half] = ya.astype(BF16)
            else:
                o_ref[rows, n0:n0 + WEIGHT_COLS] = acc.astype(BF16)


def _proj(h, w, col0, ncols, act, *, tm, rope=None, seq_len=None, col_scale=None,
          norm=None, sides=(), mod_job=None, name="proj"):
    m, k = h.shape
    steps = m // tm
    sub = ROW_GROUP.get(name, SUB_ROWS)
    in_specs, args = [], []
    if norm is not None:
        g, mod3, shift_blk, scale_blk = norm
        per = seq_len // tm
        in_specs += [
            pl.BlockSpec((tm, k), lambda i: (i, 0)),
            pl.BlockSpec((1, k), lambda i: (0, 0)),
            pl.BlockSpec((1, 1, k), lambda i: (i // per, 0, shift_blk)),
            pl.BlockSpec((1, 1, k), lambda i: (i // per, 0, scale_blk)),
        ]
        args += [h, g.reshape(1, k), mod3, mod3]
    else:
        in_specs.append(pl.BlockSpec((tm, k), lambda i: (i, 0)))
        args.append(h)
    w_specs, w_args = _weight_slabs(w, col0, ncols)
    in_specs += w_specs
    args += w_args
    if act == "rope":
        per_l = seq_len // tm
        spec = pl.BlockSpec((tm, MXU_N), lambda i: (i % per_l, 0))
        in_specs += [spec, spec]
        args += list(rope)
    out_specs = [pl.BlockSpec((tm, ncols), lambda i: (i, 0))]
    out_shape = [jax.ShapeDtypeStruct((m, ncols), BF16)]
    if norm is not None:
        out_specs.append(pl.BlockSpec((tm, k), lambda i: (i, 0)))
        out_shape.append(jax.ShapeDtypeStruct((m, k), BF16))
    for side in sides:
        arr, c0, c = side if isinstance(side, tuple) else (side, 0, side.shape[1])
        r = arr.shape[0]
        assert r % (steps * 16) == 0 and c0 % c == 0
        in_specs.append(pl.BlockSpec((r // steps, c), lambda i, cb=c0 // c: (i, cb)))
        args.append(arr)
        out_specs.append(pl.BlockSpec((r // steps, c), lambda i: (i, 0)))
        out_shape.append(jax.ShapeDtypeStruct((r, c), BF16))
    if mod_job is not None:
        cond, w_mod, b_mod2d, c0 = mod_job
        mc = (w_mod.shape[1] - c0) // steps
        assert mc * steps == w_mod.shape[1] - c0 and c0 % mc == 0
        in_specs += [
            pl.BlockSpec(cond.shape, lambda i: (0, 0)),
            pl.BlockSpec((w_mod.shape[0], mc), lambda i: (0, c0 // mc + i)),
            pl.BlockSpec((1, mc), lambda i: (0, c0 // mc + i)),
        ]
        args += [cond, w_mod, b_mod2d]
        out_specs.append(pl.BlockSpec((cond.shape[0], mc), lambda i: (0, i)))
        out_shape.append(jax.ShapeDtypeStruct((cond.shape[0], mc * steps), F32))
    outs = pl.pallas_call(
        functools.partial(_proj_kernel, act=act, col_scale=col_scale, tm=tm, sub=sub, ncols=ncols,
                          fused_norm=norm is not None, n_side=len(sides),
                          mod_job=mod_job is not None),
        grid=(steps,),
        in_specs=in_specs,
        out_specs=out_specs,
        out_shape=out_shape,
        compiler_params=_params("arbitrary"),
        name=name,
    )(*args)
    return outs[0] if len(outs) == 1 else outs


def _castproj_kernel(hc_ref, w_ref, wb_ref, kv_ref, *, m, lo, hi, k_blocks, scale):
    j = pl.program_id(0)
    wb_ref[...] = w_ref[...].astype(BF16)

    @pl.when(jnp.logical_and(j >= lo, j < hi))
    def _():
        mult = jnp.where(j < lo + k_blocks, scale, 1.0).astype(F32)
        for r0 in range(0, m, SUB_ROWS):
            rows = slice(r0, r0 + SUB_ROWS)
            kv_ref[rows, :] = (_dot(hc_ref[rows, :], wb_ref[...]) * mult).astype(BF16)


def _castproj(hc, w_f32, n, k_col0, kv_cols, k_cols, scale, tn=1024):
    m, k = hc.shape
    lo, hi = k_col0 // tn, (k_col0 + kv_cols) // tn
    assert hi * tn <= n
    return pl.pallas_call(
        functools.partial(_castproj_kernel, m=m, lo=lo, hi=hi, k_blocks=k_cols // tn, scale=scale),
        grid=(n // tn,),
        in_specs=[
            _resident((m, k), lambda j: (0, 0)),
            pl.BlockSpec((k, tn), lambda j: (0, j)),
        ],
        out_specs=[
            pl.BlockSpec((k, tn), lambda j: (0, j)),
            pl.BlockSpec((m, tn), lambda j: (0, jnp.clip(j - lo, 0, hi - lo - 1))),
        ],
        out_shape=[jax.ShapeDtypeStruct((k, n), BF16), jax.ShapeDtypeStruct((m, kv_cols), BF16)],
        compiler_params=_params("arbitrary"),
        name="castproj_ctx",
    )(hc, w_f32)


def _dot_t(a, b):
    return lax.dot_general(a, b, (((0,), (0,)), ((), ())), preferred_element_type=F32)


def _dot_nt(a, b):
    return lax.dot_general(a, b, (((1,), (1,)), ((), ())), preferred_element_type=F32)


def _ret_kernel(lg_ref, q_ref, k_ref, v_ref, kc_ref, vc_ref, o_ref, sf_ref, sb_ref,
                *, n_chunks, heads_per_step):
    c = RET_C
    ri = lax.broadcasted_iota(jnp.int32, (c, c), 0).astype(F32)
    ci = lax.broadcasted_iota(jnp.int32, (c, c), 1).astype(F32)
    diff = ri - ci
    zero = jnp.zeros((1, c), F32)

    def kv(kn, vn, dec):
        return _dot_t(kn * dec, vn)

    for t in range(heads_per_step):
        hd = pl.program_id(1) * heads_per_step + t
        hc = slice(t * c, (t + 1) * c)
        lgf = lg_ref[0, hd]
        lgb = lg_ref[1, hd]
        dmask = jnp.where(diff >= 0.0,
                          jnp.exp(lgf * jnp.maximum(diff, 0.0)),
                          jnp.exp(lgb * jnp.maximum(-diff, 0.0))).astype(BF16)
        qdf = jnp.exp(lgf * (ri + 1.0)).astype(BF16)
        qdb = jnp.exp(lgb * (c - ri)).astype(BF16)
        kdf = jnp.exp(lgf * (c - 1.0 - ri)).astype(BF16)
        kdb = jnp.exp(lgb * ri).astype(BF16)
        cdf = jnp.exp(zero + lgf * c)
        cdb = jnp.exp(zero + lgb * c)

        kc = kc_ref[:, hc]
        vc = vc_ref[:, hc]
        sf_ref[t] = kv(kc, vc, kdf)
        sb = kv(kc, vc, kdb)
        for n in reversed(range(n_chunks)):
            rows = slice(n * c, (n + 1) * c)
            sb_ref[t, n] = sb.astype(BF16)
            if n > 0:
                sb = sb * cdb + kv(k_ref[rows, hc], v_ref[rows, hc], kdb)
        for n in range(n_chunks):
            rows = slice(n * c, (n + 1) * c)
            qn = q_ref[rows, hc]
            kn = k_ref[rows, hc]
            vn = v_ref[rows, hc]
            a = _dot_nt(qn, kn).astype(BF16) * dmask
            o = (_dot(a, vn)
                 + _dot(qn * qdf, sf_ref[t].astype(BF16))
                 + _dot(qn * qdb, sb_ref[t, n]))
            o_ref[rows, hc] = _rms(o).astype(BF16)
            if n + 1 < n_chunks:
                sf_ref[t] = sf_ref[t] * cdf + kv(kn, vn, kdf)


def _retention(lg, qk, v, kvc, batch, seq_len, ctx_len, heads_per_step=RET_HEADS_PER_STEP):
    m = qk.shape[0]
    hd = RET_HEADS
    dk = qk.shape[1] // (2 * hd)
    dv = v.shape[1] // hd
    assert dk == RET_C and dv == RET_C and ctx_len == RET_C and seq_len % RET_C == 0
    n_chunks = seq_len // RET_C
    hps = heads_per_step
    hb = hd // hps
    return pl.pallas_call(
        functools.partial(_ret_kernel, n_chunks=n_chunks, heads_per_step=hps),
        grid=(batch, hb),
        in_specs=[
            pl.BlockSpec(memory_space=pltpu.SMEM),
            pl.BlockSpec((seq_len, hps * dk), lambda b, h: (b, h)),
            pl.BlockSpec((seq_len, hps * dk), lambda b, h: (b, hb + h)),
            pl.BlockSpec((seq_len, hps * dv), lambda b, h: (b, h)),
            pl.BlockSpec((ctx_len, hps * dk), lambda b, h: (b, h)),
            pl.BlockSpec((ctx_len, hps * dv), lambda b, h: (b, hb + h)),
        ],
        out_specs=pl.BlockSpec((seq_len, hps * dv), lambda b, h: (b, h)),
        out_shape=jax.ShapeDtypeStruct((m, hd * dv), BF16),
        scratch_shapes=[
            pltpu.VMEM((hps, dk, dv), F32),
            pltpu.VMEM((hps, n_chunks, dk, dv), BF16),
        ],
        compiler_params=_params("arbitrary", "arbitrary"),
        name="retention",
    )(lg, qk, qk, v, kvc, kvc)


def _branch_kernel(ret_ref, sw_ref, u_ref, vs_ref, gr_ref, gs_ref, lng_ref, lnb_ref, sgw_ref,
                   sgb_ref, *rest, tm, sub):
    o_ref, sgo_ref = rest[-2:]
    n_slabs = (len(rest) - 2) // 2
    wr_refs, ws_refs = rest[:n_slabs], rest[n_slabs:2 * n_slabs]
    width = vs_ref.shape[1]
    gd = width // SG_GROUPS
    for r0 in range(0, tm, sub):
        rows = slice(r0, r0 + sub)
        vs = vs_ref[rows, :].astype(F32)
        mu = jnp.mean(vs, axis=-1, keepdims=True)
        cen = vs - mu
        var = jnp.mean(cen * cen, axis=-1, keepdims=True)
        vn = ((cen * lax.rsqrt(var + EPS)) * lng_ref[...] + lnb_ref[...]).astype(BF16)
        for c0 in range(0, sub, SG_CHUNK):
            crow = slice(r0 + c0, r0 + c0 + SG_CHUNK)
            for gi in range(SG_GROUPS):
                cols = slice(gi * gd, (gi + 1) * gd)
                mixed = _dot(sgw_ref[gi].astype(BF16), vn[c0:c0 + SG_CHUNK, cols]) + sgb_ref[:, cols]
                sgo_ref[crow, cols] = u_ref[crow, cols] * mixed.astype(BF16)
        ret = ret_ref[rows, :] * sw_ref[rows, :]
        sgo = sgo_ref[rows, :]
        for t in range(n_slabs):
            cols = slice(t * WEIGHT_COLS, (t + 1) * WEIGHT_COLS)
            yr = _dot(ret, wr_refs[t][...])
            ys = _dot(sgo, ws_refs[t][...])
            o_ref[rows, cols] = (gr_ref[rows, cols] * yr.astype(BF16)
                                 + gs_ref[rows, cols] * ys.astype(BF16))


def _branches(retn, swish, uvs, gates, ln_g, ln_b, sgw, sgb_tab, wr, ws, tm):
    m, w = retn.shape
    d = wr.shape[1]
    row = lambda i: (i, 0)
    row1 = lambda i: (i, 1)
    const2 = lambda i: (0, 0)
    wr_specs, wr_args = _weight_slabs(wr, 0, d)
    ws_specs, ws_args = _weight_slabs(ws, 0, d)
    return pl.pallas_call(
        functools.partial(_branch_kernel, tm=tm, sub=ROW_GROUP.get("branches", SUB_ROWS)),
        grid=(m // tm,),
        in_specs=[
            pl.BlockSpec((tm, w), row),
            pl.BlockSpec((tm, w), row),
            pl.BlockSpec((tm, w), row),
            pl.BlockSpec((tm, w), row1),
            pl.BlockSpec((tm, d), row),
            pl.BlockSpec((tm, d), row1),
            _resident((1, w), const2),
            _resident((1, w), const2),
            _resident(sgw.shape, lambda i: (0, 0, 0)),
            _resident(sgb_tab.shape, const2),
        ] + wr_specs + ws_specs,
        out_specs=pl.BlockSpec((tm, d), row),
        out_shape=jax.ShapeDtypeStruct((m, d), BF16),
        scratch_shapes=[pltpu.VMEM((tm, w), BF16)],
        compiler_params=_params("arbitrary"),
        name="branches",
    )(retn, swish, uvs, uvs, gates, gates, ln_g.reshape(1, w), ln_b.reshape(1, w), sgw, sgb_tab,
      *wr_args, *ws_args)


def _outproj_kernel(m_ref, x_ref, gt1_ref, g2_ref, sh2_ref, sc2_ref, *rest, tm, sub):
    x1_ref, h2_ref = rest[-2:]
    wo_refs = rest[:-2]
    gain = g2_ref[...] * (1.0 + sc2_ref[0])
    for r0 in range(0, tm, sub):
        rows = slice(r0, r0 + sub)
        mr = m_ref[rows, :]
        for t, wo_ref in enumerate(wo_refs):
            cols = slice(t * WEIGHT_COLS, (t + 1) * WEIGHT_COLS)
            x1_ref[rows, cols] = x_ref[rows, cols] + gt1_ref[0, :, cols] * _dot(mr, wo_ref[...])
        h2_ref[rows, :] = _modulated_norm(x1_ref[rows, :], gain, sh2_ref[0])


def _outproj(merged, wo, x2d, mod3, mod_blks, g2, seq_len, tm):
    m, d = x2d.shape
    per = seq_len // tm
    row = lambda i: (i, 0)
    modspec = lambda blk: pl.BlockSpec((1, 1, d), lambda i: (i // per, 0, blk))
    wo_specs, wo_args = _weight_slabs(wo, 0, d)
    return pl.pallas_call(
        functools.partial(_outproj_kernel, tm=tm, sub=ROW_GROUP.get("outproj", SUB_ROWS)),
        grid=(m // tm,),
        in_specs=[
            pl.BlockSpec((tm, d), row),
            pl.BlockSpec((tm, d), row),
            modspec(mod_blks[0]), pl.BlockSpec((1, d), lambda i: (0, 0)),
            modspec(mod_blks[1]), modspec(mod_blks[2]),
        ] + wo_specs,
        out_specs=[pl.BlockSpec((tm, d), row), pl.BlockSpec((tm, d), row)],
        out_shape=[jax.ShapeDtypeStruct((m, d), F32), jax.ShapeDtypeStruct((m, d), BF16)],
        compiler_params=_params("arbitrary"),
        name="outproj",
    )(merged, x2d, mod3, g2.reshape(1, d), mod3, mod3, *wo_args)


def _ffn_up_kernel(h_ref, wa_ref, wb_ref, o_ref, *, tm, sub):
    for r0 in range(0, tm, sub):
        rows = slice(r0, r0 + sub)
        hr = h_ref[rows, :]
        a = _dot(hr, wa_ref[...])
        b = _dot(hr, wb_ref[...])
        o_ref[rows, :] = (_silu(a) * b).astype(BF16)


def _ffn_up(h2, w_in, tm, tn):
    m, d = h2.shape
    hidden = w_in.shape[1] // 2
    nt = hidden // tn
    return pl.pallas_call(
        functools.partial(_ffn_up_kernel, tm=tm, sub=ROW_GROUP.get("ffn_up", SUB_ROWS)),
        grid=(m // tm, nt),
        in_specs=[
            pl.BlockSpec((tm, d), lambda i, j: (i, 0)),
            pl.BlockSpec((d, tn), lambda i, j: (0, j)),
            pl.BlockSpec((d, tn), lambda i, j: (0, nt + j)),
        ],
        out_specs=pl.BlockSpec((tm, tn), lambda i, j: (i, j)),
        out_shape=jax.ShapeDtypeStruct((m, hidden), BF16),
        compiler_params=_params("arbitrary", "arbitrary"),
        name="ffn_up",
    )(h2, w_in, w_in)


def _ffn_down_kernel(a_ref, x1_ref, gt2_ref, gf_ref, *rest, tm, sub):
    o_ref = rest[-1]
    w_refs = rest[:-1]
    for r0 in range(0, tm, sub):
        rows = slice(r0, r0 + sub)
        ar = a_ref[rows, :]
        for t, w_ref in enumerate(w_refs):
            cols = slice(t * WEIGHT_COLS, (t + 1) * WEIGHT_COLS)
            o_ref[rows, cols] = x1_ref[rows, cols] + gt2_ref[0, :, cols] * _dot(ar, w_ref[...])
        o_ref[rows, :] = _rms(o_ref[rows, :]) * gf_ref[...]


def _ffn_down(act, w_out, x1, mod3, mod_blk, gf, seq_len, tm):
    m, d = x1.shape
    hidden = act.shape[1]
    per = seq_len // tm
    row = lambda i: (i, 0)
    w_specs, w_args = _weight_slabs(w_out, 0, d)
    return pl.pallas_call(
        functools.partial(_ffn_down_kernel, tm=tm, sub=ROW_GROUP.get("ffn_down", SUB_ROWS)),
        grid=(m // tm,),
        in_specs=[
            pl.BlockSpec((tm, hidden), row),
            pl.BlockSpec((tm, d), row),
            pl.BlockSpec((1, 1, d), lambda i: (i // per, 0, mod_blk)),
            pl.BlockSpec((1, d), lambda i: (0, 0)),
        ] + w_specs,
        out_specs=pl.BlockSpec((tm, d), row),
        out_shape=jax.ShapeDtypeStruct((m, d), F32),
        compiler_params=_params("arbitrary"),
        name="ffn_down",
    )(act, x1, mod3, gf.reshape(1, d), *w_args)


def _rope_tables(seq_len, dk):
    freqs = dk // 4
    rows = seq_len // GRID_W
    row = np.repeat(np.arange(rows), GRID_W)
    col = np.tile(np.arange(GRID_W), rows)
    freq = ROPE_BASE ** (-np.arange(freqs, dtype=np.float64) / freqs)
    ang = np.stack([row, col], axis=-1).astype(np.float64)[:, :, None] * freq
    cos, sin = np.cos(ang), np.sin(ang)
    cos_t = np.concatenate([cos[:, 0], cos[:, 0], cos[:, 1], cos[:, 1]], axis=-1)
    sin_t = np.concatenate([-sin[:, 0], sin[:, 0], -sin[:, 1], sin[:, 1]], axis=-1)
    return jnp.asarray(cos_t, F32), jnp.asarray(sin_t, F32)


def kernel(x, c, ctx, c_ctx, w_mod, b_mod, norm1_g, w_in, ret_decay_fwd, ret_decay_bwd,
           sg_ln_g, sg_ln_b, sg_w, sg_b, w_ret_o, w_sg_o, w_out, norm2_g, w_ffn_in, w_ffn_out,
           final_norm_g):
    batch, seq_len, d = x.shape
    ctx_len = ctx.shape[1]
    depth = w_mod.shape[0]
    assert depth == 1
    width = w_ret_o.shape[1]
    dk = width // RET_HEADS
    q_off, k_off, v_off, g_off, u_off, gr_off = (i * width for i in (0, 1, 2, 3, 4, 6))

    x2d = x.reshape(batch * seq_len, d)
    ctx2d = ctx.reshape(batch * ctx_len, d)

    pad = (-(batch + 1)) % 8
    cond = jnp.concatenate([c, c_ctx[None], jnp.zeros((pad, d), F32)], axis=0)
    b_mod2d = b_mod[0].reshape(1, -1)
    mod_a = _adaln(cond, w_mod[0], b_mod2d, 2 * d)
    mod_a3 = mod_a.reshape(mod_a.shape[0], 1, mod_a.shape[1])

    lg = jnp.stack([-jax.nn.softplus(-ret_decay_fwd[0].astype(F32)),
                    -jax.nn.softplus(-ret_decay_bwd[0].astype(F32))])

    hc = _prenorm(ctx2d, norm1_g[0], mod_a3, ctx2d.shape[0], lambda b: batch, 0, 1,
                  tl=ROW_TILE["ctx_norm"])
    w_qkv_b, kvc = _castproj(hc, w_in[0], g_off, k_off, 2 * width, width, dk ** -0.5)

    rope = _rope_tables(seq_len, dk)
    qk, h, w_g_b, w_uvs_b, w_gates_b = _proj(
        x2d, w_qkv_b, q_off, 2 * width, "rope", tm=ROW_TILE["proj_qk"], rope=rope, seq_len=seq_len,
        col_scale=(width, 2 * width, dk ** -0.5), norm=(norm1_g[0], mod_a3, 0, 1),
        sides=((w_in[0], g_off, width), (w_in[0], u_off, 2 * width), (w_in[0], gr_off, 2 * d)),
        name="proj_qk")
    v, wr_b, ws_b, wo_b = _proj(h, w_qkv_b, v_off, width, "none", tm=ROW_TILE["proj_v"],
                                sides=(w_ret_o[0], w_sg_o[0], w_out[0]), name="proj_v")
    g, w_down_b = _proj(h, w_g_b, 0, width, "silu", tm=ROW_TILE["proj_g"],
                        sides=(w_ffn_out[0],), name="proj_g")
    uvs, mod_b = _proj(h, w_uvs_b, 0, 2 * width, "gelu", tm=ROW_TILE["proj_uvs"],
                       mod_job=(cond, w_mod[0], b_mod2d, 2 * d), name="proj_uvs")
    mod_b3 = mod_b.reshape(mod_b.shape[0], 1, mod_b.shape[1])
    gates, w_up_b = _proj(h, w_gates_b, 0, 2 * d, "sigmoid", tm=ROW_TILE["proj_gates"],
                          sides=(w_ffn_in[0],), name="proj_gates")

    retn = _retention(lg, qk, v, kvc, batch, seq_len, ctx_len)

    sgb_tab = jnp.repeat(sg_b[0].T, width // SG_GROUPS, axis=1)
    merged = _branches(retn, g, uvs, gates, sg_ln_g[0], sg_ln_b[0], sg_w[0], sgb_tab, wr_b, ws_b,
                       tm=ROW_TILE["branches"])
    x1, h2 = _outproj(merged, wo_b, x2d, mod_b3, (0, 1, 2), norm2_g[0], seq_len,
                      tm=ROW_TILE["outproj"])

    act = _ffn_up(h2, w_up_b, tm=ROW_TILE["ffn_up"], tn=FFN_UP_COLS)
    out = _ffn_down(act, w_down_b, x1, mod_b3, 3, final_norm_g, seq_len,
                    tm=ROW_TILE["ffn_down"])
    return out.reshape(batch, seq_len, d)
```

```python
import functools
import math

import jax
import jax.numpy as jnp
import numpy as np
from jax import lax
from jax.experimental import pallas as pl
from jax.experimental.pallas import tpu as pltpu

F32 = jnp.float32
BF16 = jnp.bfloat16

EPS = 1e-6
GRID_W = 64
ROPE_BASE = 10000.0
RET_HEADS = 8
SG_GROUPS = 8
SG_CHUNK = 128
V7X_VMEM_BYTES = 64 * 1024 * 1024
VMEM_LIMIT = V7X_VMEM_BYTES - 8 * 1024 * 1024
MXU_N = 256
RET_C = MXU_N
SUB_ROWS = 256
WEIGHT_COLS = 512
ROW_TILE = dict(ctx_norm=1024, proj_qk=512, proj_v=1024, proj_g=1024, proj_uvs=1024,
                proj_gates=512, branches=512, outproj=512, ffn_up=4096, ffn_down=512)
FFN_UP_COLS = 512
ROW_GROUP = dict(proj_v=512, branches=128, outproj=512, ffn_up=128)
RET_HEADS_PER_STEP = 2


def _params(*sem):
    return pltpu.CompilerParams(dimension_semantics=sem, vmem_limit_bytes=VMEM_LIMIT)


def _sigmoid(x):
    return 1.0 / (1.0 + jnp.exp(-x))


def _silu(x):
    return x * _sigmoid(x)


def _gelu_tanh(x):
    b = -2.0 * math.sqrt(2.0 / math.pi) * math.log2(math.e)
    a = b * 0.044715
    return x / (1.0 + jnp.exp2(x * (a * (x * x) + b)))


def _rms(x):
    return x * lax.rsqrt(jnp.mean(x * x, axis=-1, keepdims=True) + EPS)


def _dot(a, b):
    return jnp.dot(a, b, preferred_element_type=F32)


def _resident(shape, index_map):
    return pl.BlockSpec(shape, index_map, pipeline_mode=pl.Buffered(1))


def _weight_slabs(w, col0, ncols):
    assert col0 % WEIGHT_COLS == 0 and ncols % WEIGHT_COLS == 0
    n = ncols // WEIGHT_COLS
    specs = [_resident((w.shape[0], WEIGHT_COLS), lambda i, t=t: (0, col0 // WEIGHT_COLS + t))
             for t in range(n)]
    return specs, [w] * n


def _adaln_kernel(c_ref, w_ref, b_ref, o_ref):
    s = _silu(c_ref[...]).astype(BF16)
    o_ref[...] = _dot(s, w_ref[...].astype(BF16)) + b_ref[...]


def _adaln(cond, w_mod, b_mod2d, ncols, tn=1024):
    rows, d = cond.shape
    return pl.pallas_call(
        _adaln_kernel,
        grid=(ncols // tn,),
        in_specs=[
            pl.BlockSpec((rows, d), lambda j: (0, 0)),
            pl.BlockSpec((d, tn), lambda j: (0, j)),
            pl.BlockSpec((1, tn), lambda j: (0, j)),
        ],
        out_specs=pl.BlockSpec((rows, tn), lambda j: (0, j)),
        out_shape=jax.ShapeDtypeStruct((rows, ncols), F32),
        compiler_params=_params("arbitrary"),
        name="adaln",
    )(cond, w_mod, b_mod2d)


def _modulated_norm(x, gain, sh):
    return (_rms(x) * gain + sh).astype(BF16)


def _prenorm_kernel(x_ref, g_ref, sh_ref, sc_ref, o_ref):
    o_ref[...] = _modulated_norm(x_ref[...], g_ref[...] * (1.0 + sc_ref[0]), sh_ref[0])


def _prenorm(x2d, g, mod3, rows_per_batch, mod_row, shift_blk, scale_blk, tl):
    m, d = x2d.shape
    per = rows_per_batch // tl
    return pl.pallas_call(
        _prenorm_kernel,
        grid=(m // tl,),
        in_specs=[
            pl.BlockSpec((tl, d), lambda i: (i, 0)),
            pl.BlockSpec((1, d), lambda i: (0, 0)),
            pl.BlockSpec((1, 1, d), lambda i: (mod_row(i // per), 0, shift_blk)),
            pl.BlockSpec((1, 1, d), lambda i: (mod_row(i // per), 0, scale_blk)),
        ],
        out_specs=pl.BlockSpec((tl, d), lambda i: (i, 0)),
        out_shape=jax.ShapeDtypeStruct((m, d), BF16),
        compiler_params=_params("arbitrary"),
        name="prenorm",
    )(x2d, g.reshape(1, d), mod3, mod3)


def _activate(acc, act):
    if act == "silu":
        return _silu(acc)
    if act == "gelu":
        return _gelu_tanh(acc)
    if act == "sigmoid":
        return _sigmoid(acc)
    return acc


def _proj_kernel(*refs, act, col_scale, tm, sub, ncols, fused_norm, n_side, mod_job):
    refs = list(refs)
    if mod_job:
        mo_ref = refs.pop()
    side_out = [refs.pop() for _ in range(n_side)][::-1]
    if fused_norm:
        x_ref, g_ref, sh_ref, sc_ref = refs[:4]
        refs = refs[4:]
        h_ref = refs.pop()
    else:
        h_ref = refs.pop(0)
    w_refs = [refs.pop(0) for _ in range(ncols // WEIGHT_COLS)]
    o_ref = refs.pop()
    if mod_job:
        c_ref, wm_ref, bm_ref = refs[-3:]
        refs = refs[:-3]
        _adaln_kernel(c_ref, wm_ref, bm_ref, mo_ref)
    side_in = [refs.pop() for _ in range(n_side)][::-1]
    if act == "rope":
        cos_ref, sin_ref = refs
    for src, dst in zip(side_in, side_out):
        dst[...] = src[...].astype(BF16)
    half = MXU_N // 2
    if fused_norm:
        gain = g_ref[...] * (1.0 + sc_ref[0])
    for r0 in range(0, tm, sub):
        rows = slice(r0, r0 + sub)
        if fused_norm:
            h_ref[rows, :] = _modulated_norm(x_ref[rows, :], gain, sh_ref[0])
        hr = h_ref[rows, :]
        for n0 in range(0, ncols, WEIGHT_COLS):
            acc = _dot(hr, w_refs[n0 // WEIGHT_COLS][...])
            if col_scale is not None and col_scale[0] <= n0 < col_scale[1]:
                acc = acc * col_scale[2]
            if act == "rope":
                for s in range(WEIGHT_COLS // half):
                    lanes = slice((s % 2) * half, (s % 2 + 1) * half)
                    xa = acc[:, s * half:(s + 1) * half]
                    ya = (xa * cos_ref[rows, lanes]
                          + pltpu.roll(xa, half // 2, axis=1) * sin_ref[rows, lanes])
                    o_ref[rows, n0 + s * half:n0 + (s + 1) * half] = ya.astype(BF16)
            else:
                for h0 in range(0, WEIGHT_COLS, MXU_N):
                    o_ref[rows, n0 + h0:n0 + h0 + MXU_N] = _activate(
                        acc[:, h0:h0 + MXU_N], act).astype(BF16)


def _proj(h, w, col0, ncols, act, *, tm, rope=None, seq_len=None, col_scale=None,
          norm=None, sides=(), mod_job=None, name="proj"):
    m, k = h.shape
    steps = m // tm
    sub = ROW_GROUP.get(name, SUB_ROWS)
    in_specs, args = [], []
    if norm is not None:
        g, mod3, shift_blk, scale_blk = norm
        per = seq_len // tm
        in_specs += [
            pl.BlockSpec((tm, k), lambda i: (i, 0)),
            pl.BlockSpec((1, k), lambda i: (0, 0)),
            pl.BlockSpec((1, 1, k), lambda i: (i // per, 0, shift_blk)),
            pl.BlockSpec((1, 1, k), lambda i: (i // per, 0, scale_blk)),
        ]
        args += [h, g.reshape(1, k), mod3, mod3]
    else:
        in_specs.append(pl.BlockSpec((tm, k), lambda i: (i, 0)))
        args.append(h)
    w_specs, w_args = _weight_slabs(w, col0, ncols)
    in_specs += w_specs
    args += w_args
    if act == "rope":
        per_l = seq_len // tm
        spec = pl.BlockSpec((tm, MXU_N), lambda i: (i % per_l, 0))
        in_specs += [spec, spec]
        args += list(rope)
    out_specs = [pl.BlockSpec((tm, ncols), lambda i: (i, 0))]
    out_shape = [jax.ShapeDtypeStruct((m, ncols), BF16)]
    if norm is not None:
        out_specs.append(pl.BlockSpec((tm, k), lambda i: (i, 0)))
        out_shape.append(jax.ShapeDtypeStruct((m, k), BF16))
    for side in sides:
        arr, c0, c = side if isinstance(side, tuple) else (side, 0, side.shape[1])
        r = arr.shape[0]
        assert r % (steps * 16) == 0 and c0 % c == 0
        in_specs.append(pl.BlockSpec((r // steps, c), lambda i, cb=c0 // c: (i, cb)))
        args.append(arr)
        out_specs.append(pl.BlockSpec((r // steps, c), lambda i: (i, 0)))
        out_shape.append(jax.ShapeDtypeStruct((r, c), BF16))
    if mod_job is not None:
        cond, w_mod, b_mod2d, c0 = mod_job
        mc = (w_mod.shape[1] - c0) // steps
        assert mc * steps == w_mod.shape[1] - c0 and c0 % mc == 0
        in_specs += [
            pl.BlockSpec(cond.shape, lambda i: (0, 0)),
            pl.BlockSpec((w_mod.shape[0], mc), lambda i: (0, c0 // mc + i)),
            pl.BlockSpec((1, mc), lambda i: (0, c0 // mc + i)),
        ]
        args += [cond, w_mod, b_mod2d]
        out_specs.append(pl.BlockSpec((cond.shape[0], mc), lambda i: (0, i)))
        out_shape.append(jax.ShapeDtypeStruct((cond.shape[0], mc * steps), F32))
    outs = pl.pallas_call(
        functools.partial(_proj_kernel, act=act, col_scale=col_scale, tm=tm, sub=sub, ncols=ncols,
                          fused_norm=norm is not None, n_side=len(sides),
                          mod_job=mod_job is not None),
        grid=(steps,),
        in_specs=in_specs,
        out_specs=out_specs,
        out_shape=out_shape,
        compiler_params=_params("arbitrary"),
        name=name,
    )(*args)
    return outs[0] if len(outs) == 1 else outs


def _castproj_kernel(hc_ref, w_ref, wb_ref, kv_ref, *, m, lo, hi, k_blocks, scale):
    j = pl.program_id(0)
    wb_ref[...] = w_ref[...].astype(BF16)

    @pl.when(jnp.logical_and(j >= lo, j < hi))
    def _():
        mult = jnp.where(j < lo + k_blocks, scale, 1.0).astype(F32)
        for r0 in range(0, m, SUB_ROWS):
            rows = slice(r0, r0 + SUB_ROWS)
            kv_ref[rows, :] = (_dot(hc_ref[rows, :], wb_ref[...]) * mult).astype(BF16)


def _castproj(hc, w_f32, n, k_col0, kv_cols, k_cols, scale, tn=1024):
    m, k = hc.shape
    lo, hi = k_col0 // tn, (k_col0 + kv_cols) // tn
    assert hi * tn <= n
    return pl.pallas_call(
        functools.partial(_castproj_kernel, m=m, lo=lo, hi=hi, k_blocks=k_cols // tn, scale=scale),
        grid=(n // tn,),
        in_specs=[
            _resident((m, k), lambda j: (0, 0)),
            pl.BlockSpec((k, tn), lambda j: (0, j)),
        ],
        out_specs=[
            pl.BlockSpec((k, tn), lambda j: (0, j)),
            pl.BlockSpec((m, tn), lambda j: (0, jnp.clip(j - lo, 0, hi - lo - 1))),
        ],
        out_shape=[jax.ShapeDtypeStruct((k, n), BF16), jax.ShapeDtypeStruct((m, kv_cols), BF16)],
        compiler_params=_params("arbitrary"),
        name="castproj_ctx",
    )(hc, w_f32)


def _dot_t(a, b):
    return lax.dot_general(a, b, (((0,), (0,)), ((), ())), preferred_element_type=F32)


def _dot_nt(a, b):
    return lax.dot_general(a, b, (((1,), (1,)), ((), ())), preferred_element_type=F32)


def _ret_kernel(lg_ref, q_ref, k_ref, v_ref, kc_ref, vc_ref, o_ref, sf_ref, sb_ref,
                *, n_chunks, heads_per_step):
    c = RET_C
    ri = lax.broadcasted_iota(jnp.int32, (c, c), 0).astype(F32)
    ci = lax.broadcasted_iota(jnp.int32, (c, c), 1).astype(F32)
    diff = ri - ci
    zero = jnp.zeros((1, c), F32)

    def kv(kn, vn, dec):
        return _dot_t(kn * dec, vn)

    for t in range(heads_per_step):
        hd = pl.program_id(1) * heads_per_step + t
        hc = slice(t * c, (t + 1) * c)
        lgf = lg_ref[0, hd]
        lgb = lg_ref[1, hd]
        dmask = jnp.where(diff >= 0.0,
                          jnp.exp(lgf * jnp.maximum(diff, 0.0)),
                          jnp.exp(lgb * jnp.maximum(-diff, 0.0))).astype(BF16)
        qdf = jnp.exp(lgf * (ri + 1.0)).astype(BF16)
        qdb = jnp.exp(lgb * (c - ri)).astype(BF16)
        kdf = jnp.exp(lgf * (c - 1.0 - ri)).astype(BF16)
        kdb = jnp.exp(lgb * ri).astype(BF16)
        cdf = jnp.exp(zero + lgf * c)
        cdb = jnp.exp(zero + lgb * c)

        kc = kc_ref[:, hc]
        vc = vc_ref[:, hc]
        sf_ref[t] = kv(kc, vc, kdf)
        sb = kv(kc, vc, kdb)
        for n in reversed(range(n_chunks)):
            rows = slice(n * c, (n + 1) * c)
            sb_ref[t, n] = sb.astype(BF16)
            if n > 0:
                sb = sb * cdb + kv(k_ref[rows, hc], v_ref[rows, hc], kdb)
        for n in range(n_chunks):
            rows = slice(n * c, (n + 1) * c)
            qn = q_ref[rows, hc]
            kn = k_ref[rows, hc]
            vn = v_ref[rows, hc]
            a = _dot_nt(qn, kn).astype(BF16) * dmask
            o = (_dot(a, vn)
                 + _dot(qn * qdf, sf_ref[t].astype(BF16))
                 + _dot(qn * qdb, sb_ref[t, n]))
            o_ref[rows, hc] = _rms(o).astype(BF16)
            if n + 1 < n_chunks:
                sf_ref[t] = sf_ref[t] * cdf + kv(kn, vn, kdf)


def _retention(lg, qk, v, kvc, batch, seq_len, ctx_len, heads_per_step=RET_HEADS_PER_STEP):
    m = qk.shape[0]
    hd = RET_HEADS
    dk = qk.shape[1] // (2 * hd)
    dv = v.shape[1] // hd
    assert dk == RET_C and dv == RET_C and ctx_len == RET_C and seq_len % RET_C == 0
    n_chunks = seq_len // RET_C
    hps = heads_per_step
    hb = hd // hps
    return pl.pallas_call(
        functools.partial(_ret_kernel, n_chunks=n_chunks, heads_per_step=hps),
        grid=(batch, hb),
        in_specs=[
            pl.BlockSpec(memory_space=pltpu.SMEM),
            pl.BlockSpec((seq_len, hps * dk), lambda b, h: (b, h)),
            pl.BlockSpec((seq_len, hps * dk), lambda b, h: (b, hb + h)),
            pl.BlockSpec((seq_len, hps * dv), lambda b, h: (b, h)),
            pl.BlockSpec((ctx_len, hps * dk), lambda b, h: (b, h)),
            pl.BlockSpec((ctx_len, hps * dv), lambda b, h: (b, hb + h)),
        ],
        out_specs=pl.BlockSpec((seq_len, hps * dv), lambda b, h: (b, h)),
        out_shape=jax.ShapeDtypeStruct((m, hd * dv), BF16),
        scratch_shapes=[
            pltpu.VMEM((hps, dk, dv), F32),
            pltpu.VMEM((hps, n_chunks, dk, dv), BF16),
        ],
        compiler_params=_params("arbitrary", "arbitrary"),
        name="retention",
    )(lg, qk, qk, v, kvc, kvc)


def _branch_kernel(ret_ref, sw_ref, u_ref, vs_ref, gr_ref, gs_ref, lng_ref, lnb_ref, sgw_ref,
                   sgb_ref, *rest, tm, sub):
    o_ref, sgo_ref = rest[-2:]
    n_slabs = (len(rest) - 2) // 2
    wr_refs, ws_refs = rest[:n_slabs], rest[n_slabs:2 * n_slabs]
    width = vs_ref.shape[1]
    gd = width // SG_GROUPS
    for r0 in range(0, tm, sub):
        rows = slice(r0, r0 + sub)
        vs = vs_ref[rows, :].astype(F32)
        mu = jnp.mean(vs, axis=-1, keepdims=True)
        cen = vs - mu
        var = jnp.mean(cen * cen, axis=-1, keepdims=True)
        vn = ((cen * lax.rsqrt(var + EPS)) * lng_ref[...] + lnb_ref[...]).astype(BF16)
        for c0 in range(0, sub, SG_CHUNK):
            crow = slice(r0 + c0, r0 + c0 + SG_CHUNK)
            for gi in range(SG_GROUPS):
                cols = slice(gi * gd, (gi + 1) * gd)
                mixed = _dot(sgw_ref[gi].astype(BF16), vn[c0:c0 + SG_CHUNK, cols]) + sgb_ref[:, cols]
                sgo_ref[crow, cols] = u_ref[crow, cols] * mixed.astype(BF16)
        ret = ret_ref[rows, :] * sw_ref[rows, :]
        sgo = sgo_ref[rows, :]
        for t in range(n_slabs):
            cols = slice(t * WEIGHT_COLS, (t + 1) * WEIGHT_COLS)
            yr = _dot(ret, wr_refs[t][...])
            ys = _dot(sgo, ws_refs[t][...])
            o_ref[rows, cols] = (gr_ref[rows, cols] * yr.astype(BF16)
                                 + gs_ref[rows, cols] * ys.astype(BF16))


def _branches(retn, swish, uvs, gates, ln_g, ln_b, sgw, sgb_tab, wr, ws, tm):
    m, w = retn.shape
    d = wr.shape[1]
    row = lambda i: (i, 0)
    row1 = lambda i: (i, 1)
    const2 = lambda i: (0, 0)
    wr_specs, wr_args = _weight_slabs(wr, 0, d)
    ws_specs, ws_args = _weight_slabs(ws, 0, d)
    return pl.pallas_call(
        functools.partial(_branch_kernel, tm=tm, sub=ROW_GROUP.get("branches", SUB_ROWS)),
        grid=(m // tm,),
        in_specs=[
            pl.BlockSpec((tm, w), row),
            pl.BlockSpec((tm, w), row),
            pl.BlockSpec((tm, w), row),
            pl.BlockSpec((tm, w), row1),
            pl.BlockSpec((tm, d), row),
            pl.BlockSpec((tm, d), row1),
            _resident((1, w), const2),
            _resident((1, w), const2),
            _resident(sgw.shape, lambda i: (0, 0, 0)),
            _resident(sgb_tab.shape, const2),
        ] + wr_specs + ws_specs,
        out_specs=pl.BlockSpec((tm, d), row),
        out_shape=jax.ShapeDtypeStruct((m, d), BF16),
        scratch_shapes=[pltpu.VMEM((tm, w), BF16)],
        compiler_params=_params("arbitrary"),
        name="branches",
    )(retn, swish, uvs, uvs, gates, gates, ln_g.reshape(1, w), ln_b.reshape(1, w), sgw, sgb_tab,
      *wr_args, *ws_args)


def _outproj_kernel(m_ref, x_ref, gt1_ref, g2_ref, sh2_ref, sc2_ref, *rest, tm, sub):
    x1_ref, h2_ref = rest[-2:]
    wo_refs = rest[:-2]
    gain = g2_ref[...] * (1.0 + sc2_ref[0])
    for r0 in range(0, tm, sub):
        rows = slice(r0, r0 + sub)
        mr = m_ref[rows, :]
        for t, wo_ref in enumerate(wo_refs):
            cols = slice(t * WEIGHT_COLS, (t + 1) * WEIGHT_COLS)
            x1_ref[rows, cols] = x_ref[rows, cols] + gt1_ref[0, :, cols] * _dot(mr, wo_ref[...])
        h2_ref[rows, :] = _modulated_norm(x1_ref[rows, :], gain, sh2_ref[0])


def _outproj(merged, wo, x2d, mod3, mod_blks, g2, seq_len, tm):
    m, d = x2d.shape
    per = seq_len // tm
    row = lambda i: (i, 0)
    modspec = lambda blk: pl.BlockSpec((1, 1, d), lambda i: (i // per, 0, blk))
    wo_specs, wo_args = _weight_slabs(wo, 0, d)
    return pl.pallas_call(
        functools.partial(_outproj_kernel, tm=tm, sub=ROW_GROUP.get("outproj", SUB_ROWS)),
        grid=(m // tm,),
        in_specs=[
            pl.BlockSpec((tm, d), row),
            pl.BlockSpec((tm, d), row),
            modspec(mod_blks[0]), pl.BlockSpec((1, d), lambda i: (0, 0)),
            modspec(mod_blks[1]), modspec(mod_blks[2]),
        ] + wo_specs,
        out_specs=[pl.BlockSpec((tm, d), row), pl.BlockSpec((tm, d), row)],
        out_shape=[jax.ShapeDtypeStruct((m, d), F32), jax.ShapeDtypeStruct((m, d), BF16)],
        compiler_params=_params("arbitrary"),
        name="outproj",
    )(merged, x2d, mod3, g2.reshape(1, d), mod3, mod3, *wo_args)


def _ffn_up_kernel(h_ref, wa_ref, wb_ref, o_ref, *, tm, sub):
    for r0 in range(0, tm, sub):
        rows = slice(r0, r0 + sub)
        hr = h_ref[rows, :]
        a = _dot(hr, wa_ref[...])
        b = _dot(hr, wb_ref[...])
        o_ref[rows, :] = (_silu(a) * b).astype(BF16)


def _ffn_up(h2, w_in, tm, tn):
    m, d = h2.shape
    hidden = w_in.shape[1] // 2
    nt = hidden // tn
    return pl.pallas_call(
        functools.partial(_ffn_up_kernel, tm=tm, sub=ROW_GROUP.get("ffn_up", SUB_ROWS)),
        grid=(m // tm, nt),
        in_specs=[
            pl.BlockSpec((tm, d), lambda i, j: (i, 0)),
            pl.BlockSpec((d, tn), lambda i, j: (0, j)),
            pl.BlockSpec((d, tn), lambda i, j: (0, nt + j)),
        ],
        out_specs=pl.BlockSpec((tm, tn), lambda i, j: (i, j)),
        out_shape=jax.ShapeDtypeStruct((m, hidden), BF16),
        compiler_params=_params("arbitrary", "arbitrary"),
        name="ffn_up",
    )(h2, w_in, w_in)


def _ffn_down_kernel(a_ref, x1_ref, gt2_ref, gf_ref, *rest, tm, sub):
    o_ref = rest[-1]
    w_refs = rest[:-1]
    for r0 in range(0, tm, sub):
        rows = slice(r0, r0 + sub)
        ar = a_ref[rows, :]
        for t, w_ref in enumerate(w_refs):
            cols = slice(t * WEIGHT_COLS, (t + 1) * WEIGHT_COLS)
            o_ref[rows, cols] = x1_ref[rows, cols] + gt2_ref[0, :, cols] * _dot(ar, w_ref[...])
        o_ref[rows, :] = _rms(o_ref[rows, :]) * gf_ref[...]


def _ffn_down(act, w_out, x1, mod3, mod_blk, gf, seq_len, tm):
    m, d = x1.shape
    hidden = act.shape[1]
    per = seq_len // tm
    row = lambda i: (i, 0)
    w_specs, w_args = _weight_slabs(w_out, 0, d)
    return pl.pallas_call(
        functools.partial(_ffn_down_kernel, tm=tm, sub=ROW_GROUP.get("ffn_down", SUB_ROWS)),
        grid=(m // tm,),
        in_specs=[
            pl.BlockSpec((tm, hidden), row),
            pl.BlockSpec((tm, d), row),
            pl.BlockSpec((1, 1, d), lambda i: (i // per, 0, mod_blk)),
            pl.BlockSpec((1, d), lambda i: (0, 0)),
        ] + w_specs,
        out_specs=pl.BlockSpec((tm, d), row),
        out_shape=jax.ShapeDtypeStruct((m, d), F32),
        compiler_params=_params("arbitrary"),
        name="ffn_down",
    )(act, x1, mod3, gf.reshape(1, d), *w_args)


def _rope_tables(seq_len, dk):
    freqs = dk // 4
    rows = seq_len // GRID_W
    row = np.repeat(np.arange(rows), GRID_W)
    col = np.tile(np.arange(GRID_W), rows)
    freq = ROPE_BASE ** (-np.arange(freqs, dtype=np.float64) / freqs)
    ang = np.stack([row, col], axis=-1).astype(np.float64)[:, :, None] * freq
    cos, sin = np.cos(ang), np.sin(ang)
    cos_t = np.concatenate([cos[:, 0], cos[:, 0], cos[:, 1], cos[:, 1]], axis=-1)
    sin_t = np.concatenate([-sin[:, 0], sin[:, 0], -sin[:, 1], sin[:, 1]], axis=-1)
    return jnp.asarray(cos_t, F32), jnp.asarray(sin_t, F32)


def kernel(x, c, ctx, c_ctx, w_mod, b_mod, norm1_g, w_in, ret_decay_fwd, ret_decay_bwd,
           sg_ln_g, sg_ln_b, sg_w, sg_b, w_ret_o, w_sg_o, w_out, norm2_g, w_ffn_in, w_ffn_out,
           final_norm_g):
    batch, seq_len, d = x.shape
    ctx_len = ctx.shape[1]
    depth = w_mod.shape[0]
    assert depth == 1
    width = w_ret_o.shape[1]
    dk = width // RET_HEADS
    q_off, k_off, v_off, g_off, u_off, gr_off = (i * width for i in (0, 1, 2, 3, 4, 6))

    x2d = x.reshape(batch * seq_len, d)
    ctx2d = ctx.reshape(batch * ctx_len, d)

    pad = (-(batch + 1)) % 8
    cond = jnp.concatenate([c, c_ctx[None], jnp.zeros((pad, d), F32)], axis=0)
    b_mod2d = b_mod[0].reshape(1, -1)
    mod_a = _adaln(cond, w_mod[0], b_mod2d, 2 * d)
    mod_a3 = mod_a.reshape(mod_a.shape[0], 1, mod_a.shape[1])

    lg = jnp.stack([-jax.nn.softplus(-ret_decay_fwd[0].astype(F32)),
                    -jax.nn.softplus(-ret_decay_bwd[0].astype(F32))])

    hc = _prenorm(ctx2d, norm1_g[0], mod_a3, ctx2d.shape[0], lambda b: batch, 0, 1,
                  tl=ROW_TILE["ctx_norm"])
    w_qkv_b, kvc = _castproj(hc, w_in[0], g_off, k_off, 2 * width, width, dk ** -0.5)

    rope = _rope_tables(seq_len, dk)
    qk, h, w_g_b, w_uvs_b, w_gates_b = _proj(
        x2d, w_qkv_b, q_off, 2 * width, "rope", tm=ROW_TILE["proj_qk"], rope=rope, seq_len=seq_len,
        col_scale=(width, 2 * width, dk ** -0.5), norm=(norm1_g[0], mod_a3, 0, 1),
        sides=((w_in[0], g_off, width), (w_in[0], u_off, 2 * width), (w_in[0], gr_off, 2 * d)),
        name="proj_qk")
    v, wr_b, ws_b, wo_b = _proj(h, w_qkv_b, v_off, width, "none", tm=ROW_TILE["proj_v"],
                                sides=(w_ret_o[0], w_sg_o[0], w_out[0]), name="proj_v")
    g, w_down_b = _proj(h, w_g_b, 0, width, "silu", tm=ROW_TILE["proj_g"],
                        sides=(w_ffn_out[0],), name="proj_g")
    uvs, mod_b = _proj(h, w_uvs_b, 0, 2 * width, "gelu", tm=ROW_TILE["proj_uvs"],
                       mod_job=(cond, w_mod[0], b_mod2d, 2 * d), name="proj_uvs")
    mod_b3 = mod_b.reshape(mod_b.shape[0], 1, mod_b.shape[1])
    gates, w_up_b = _proj(h, w_gates_b, 0, 2 * d, "sigmoid", tm=ROW_TILE["proj_gates"],
                          sides=(w_ffn_in[0],), name="proj_gates")

    retn = _retention(lg, qk, v, kvc, batch, seq_len, ctx_len)

    sgb_tab = jnp.repeat(sg_b[0].T, width // SG_GROUPS, axis=1)
    merged = _branches(retn, g, uvs, gates, sg_ln_g[0], sg_ln_b[0], sg_w[0], sgb_tab, wr_b, ws_b,
                       tm=ROW_TILE["branches"])
    x1, h2 = _outproj(merged, wo_b, x2d, mod_b3, (0, 1, 2), norm2_g[0], seq_len,
                      tm=ROW_TILE["outproj"])

    act = _ffn_up(h2, w_up_b, tm=ROW_TILE["ffn_up"], tn=FFN_UP_COLS)
    out = _ffn_down(act, w_down_b, x1, mod_b3, 3, final_norm_g, seq_len,
                    tm=ROW_TILE["ffn_down"])
    return out.reshape(batch, seq_len, d)
```
